```python
import math
import jax, jax.numpy as jnp
from jax import lax
import numpy as np

D_MODEL = 1024
BATCH = 16
SEQ = 2048
DEPTH = 2

N_MIXERS = 2
HEAD_DIM = 64
MEM_LEN = 256
MEM_HEADS = 4
MEM_W = MEM_HEADS * HEAD_DIM
TOK_W = D_MODEL - MEM_W
CHUNK = 128
GMLP_GROUP_DIM = 128
GMLP_GROUPS = TOK_W // GMLP_GROUP_DIM
DIFF_HEADS = TOK_W // (2 * HEAD_DIM)
Q_BLOCK = 128
D_FF = ((8 * D_MODEL // 3) + 127) // 128 * 128
EPS = 1e-6

kernel_name = "hybrid_gmlp_diffattn_macaron_memxattn"


def rms_norm(x, g):
    xf = x.astype(jnp.float32)
    y = xf * lax.rsqrt(jnp.mean(xf * xf, axis=-1, keepdims=True) + EPS)
    return (y * g.astype(jnp.float32)).astype(x.dtype)


def swiglu(h, w_in, w_out):
    gate, up = jnp.split(h @ w_in, 2, axis=-1)
    return (jax.nn.silu(gate) * up) @ w_out


def chunked_gmlp(z, v_gain, w_s, b_s):
    B, S, _ = z.shape
    u, v = jnp.split(jax.nn.gelu(z, approximate=False), 2, axis=-1)
    v = rms_norm(v.reshape(B, S, GMLP_GROUPS, GMLP_GROUP_DIM), v_gain.reshape(GMLP_GROUPS, GMLP_GROUP_DIM))
    v = v.reshape(B, S // CHUNK, CHUNK, GMLP_GROUPS, GMLP_GROUP_DIM)
    ws = w_s * jnp.tril(jnp.ones((CHUNK, CHUNK), dtype=w_s.dtype))
    mixed = jnp.einsum('gts,bnsgc->bntgc', ws, v) + b_s.T[:, :, None]
    return u * mixed.reshape(B, S, TOK_W)


def diff_attention(z, gq, gk, lam_p, subln_g, lambda_init):
    B, S, _ = z.shape
    q, k, v = jnp.split(z, 3, axis=-1)
    q = rms_norm(q.reshape(B, S, DIFF_HEADS, 2, HEAD_DIM), gq)
    k = rms_norm(k.reshape(B, S, DIFF_HEADS, 2, HEAD_DIM), gk)
    vf = v.reshape(B, S, DIFF_HEADS, 2 * HEAD_DIM).astype(jnp.float32)
    lp = lam_p.astype(jnp.float32)
    lam = jnp.exp(jnp.sum(lp[0] * lp[1])) - jnp.exp(jnp.sum(lp[2] * lp[3])) + lambda_init
    scale = HEAD_DIM ** -0.5
    n_blk = S // Q_BLOCK
    qb = q.reshape(B, n_blk, Q_BLOCK, DIFF_HEADS, 2, HEAD_DIM).transpose(1, 0, 2, 3, 4, 5)
    kpos = jnp.arange(S)

    def block(args):
        q_blk, i = args
        s = jnp.einsum('bqhcd,bkhcd->bhcqk', q_blk, k).astype(jnp.float32) * scale
        qpos = i * Q_BLOCK + jnp.arange(Q_BLOCK)
        mask = kpos[None, :] <= qpos[:, None]
        p = jax.nn.softmax(jnp.where(mask, s, -jnp.inf), axis=-1)
        a = p[:, :, 0] - lam * p[:, :, 1]
        return jnp.einsum('bhqk,bkhe->bqhe', a, vf)

    o = lax.map(block, (qb, jnp.arange(n_blk)))
    o = o.transpose(1, 0, 2, 3, 4).reshape(B, S, DIFF_HEADS, 2 * HEAD_DIM)
    o = rms_norm(o, subln_g) * (1.0 - lambda_init)
    return o.reshape(B, S, TOK_W).astype(z.dtype)


def mem_cross_attention(qm, mem_h, w_kv, gq, gk):
    B, S, _ = qm.shape
    L = mem_h.shape[1]
    q = rms_norm(qm.reshape(B, S, MEM_HEADS, HEAD_DIM), gq)
    k, v = jnp.split(mem_h @ w_kv, 2, axis=-1)
    k = rms_norm(k.reshape(B, L, MEM_HEADS, HEAD_DIM), gk)
    v = v.reshape(B, L, MEM_HEADS, HEAD_DIM).astype(jnp.float32)
    s = jnp.einsum('bshd,blhd->bhsl', q, k).astype(jnp.float32) * (HEAD_DIM ** -0.5)
    p = jax.nn.softmax(s, axis=-1)
    o = jnp.einsum('bhsl,blhd->bshd', p, v)
    return o.reshape(B, S, MEM_W).astype(qm.dtype)


def setup_inputs(seed: int = 0) -> dict:
    key = jax.random.key(seed)
    ks = jax.random.split(key, 24)
    n_a = (DEPTH + 1) // 2
    n_b = DEPTH // 2
    f32 = jnp.float32

    def nrm(k, shape, s):
        return jax.random.normal(k, shape, f32) * s

    def gain(k, shape):
        return 1.0 + 0.02 * jax.random.normal(k, shape, f32)

    return {
        "x": jax.random.normal(ks[0], (BATCH, SEQ, D_MODEL), f32),
        "mem": jax.random.normal(ks[1], (BATCH, MEM_LEN, D_MODEL), f32),
        "ffn_norm": gain(ks[2], (DEPTH, 2, D_MODEL)),
        "ffn_w_in": nrm(ks[3], (DEPTH, 2, D_MODEL, 2 * D_FF), D_MODEL ** -0.5),
        "ffn_w_out": nrm(ks[4], (DEPTH, 2, D_FF, D_MODEL), D_FF ** -0.5),
        "mix_norm": gain(ks[5], (DEPTH, D_MODEL)),
        "mem_norm": gain(ks[6], (DEPTH, D_MODEL)),
        "w_mem_kv": nrm(ks[7], (DEPTH, D_MODEL, 2 * MEM_W), D_MODEL ** -0.5),
        "memq_norm": gain(ks[8], (DEPTH, HEAD_DIM)),
        "memk_norm": gain(ks[9], (DEPTH, HEAD_DIM)),
        "w_out": nrm(ks[10], (DEPTH, TOK_W + MEM_W, D_MODEL), (TOK_W + MEM_W) ** -0.5),
        "a_w_in": nrm(ks[11], (n_a, D_MODEL, 2 * TOK_W + MEM_W), D_MODEL ** -0.5),
        "a_v_norm": gain(ks[12], (n_a, TOK_W)),
        "a_w_s": nrm(ks[13], (n_a, GMLP_GROUPS, CHUNK, CHUNK), CHUNK ** -0.5),
        "a_b_s": nrm(ks[14], (n_a, GMLP_GROUPS, CHUNK), 0.02),
        "b_w_in": nrm(ks[15], (n_b, D_MODEL, 3 * TOK_W + MEM_W), D_MODEL ** -0.5),
        "b_q_norm": gain(ks[16], (n_b, HEAD_DIM)),
        "b_k_norm": gain(ks[17], (n_b, HEAD_DIM)),
        "b_lambda": nrm(ks[18], (n_b, 4, HEAD_DIM), 0.1),
        "b_subln": gain(ks[19], (n_b, 2 * HEAD_DIM)),
    }


def reference(x, mem, ffn_norm, ffn_w_in, ffn_w_out, mix_norm, mem_norm, w_mem_kv,
              memq_norm, memk_norm, w_out, a_w_in, a_v_norm, a_w_s, a_b_s,
              b_w_in, b_q_norm, b_k_norm, b_lambda, b_subln):
    for i in range(DEPTH):
        j = i // N_MIXERS
        x = x + 0.5 * swiglu(rms_norm(x, ffn_norm[i, 0]), ffn_w_in[i, 0], ffn_w_out[i, 0])
        h = rms_norm(x, mix_norm[i])
        mem_h = rms_norm(mem, mem_norm[i])
        if i % N_MIXERS == 0:
            z = h @ a_w_in[j]
            tok = chunked_gmlp(z[..., :2 * TOK_W], a_v_norm[j], a_w_s[j], a_b_s[j])
            qm = z[..., 2 * TOK_W:]
        else:
            z = h @ b_w_in[j]
            lambda_init = 0.8 - 0.6 * math.exp(-0.3 * i)
            tok = diff_attention(z[..., :3 * TOK_W], b_q_norm[j], b_k_norm[j], b_lambda[j], b_subln[j], lambda_init)
            qm = z[..., 3 * TOK_W:]
        mo = mem_cross_attention(qm, mem_h, w_mem_kv[i], memq_norm[i], memk_norm[i])
        x = x + jnp.concatenate([tok, mo], axis=-1) @ w_out[i]
        x = x + 0.5 * swiglu(rms_norm(x, ffn_norm[i, 1]), ffn_w_in[i, 1], ffn_w_out[i, 1])
    return x
```

```python
import functools
import math

import jax
import jax.numpy as jnp
from jax import lax
from jax.experimental import pallas as pl
from jax.experimental.pallas import tpu as pltpu

D_MODEL = 1024
HEAD_DIM = 64
MEM_HEADS = 4
MEM_W = MEM_HEADS * HEAD_DIM
TOK_W = D_MODEL - MEM_W
CHUNK = 128
GROUP_DIM = 128
GROUPS = TOK_W // GROUP_DIM
DIFF_HEADS = TOK_W // (2 * HEAD_DIM)
VAL_DIM = 2 * HEAD_DIM
N_MIXERS = 2
EPS = 1e-6
ATTN_SCALE = HEAD_DIM ** -0.5

LANES = 128
NORM_SLAB = 256

FFN_TM = 256
FFN_CK = 256
MIX_TM = 256
ATTN_TQ = 256
ATTN_TK = 256
VMEM_LIMIT = 56 * 1024 * 1024

F32 = jnp.float32
BF16 = jnp.bfloat16


def _const_spec(shape):
    nd = len(shape)
    return pl.BlockSpec(shape, lambda *_: (0,) * nd, pipeline_mode=pl.Buffered(1))


def _rms(x, g):
    ms = jnp.mean(x * x, axis=-1, keepdims=True)
    return x * lax.rsqrt(ms + EPS) * g


def _dot(a, b):
    return jnp.dot(a, b, preferred_element_type=F32)


def _dot_nt(a, b):
    return lax.dot_general(a, b, (((1,), (1,)), ((), ())), preferred_element_type=F32)


def _block_diag_ones(width, group):
    r = lax.broadcasted_iota(jnp.int32, (width, width), 0) // group
    c = lax.broadcasted_iota(jnp.int32, (width, width), 1) // group
    return jnp.where(r == c, 1.0, 0.0).astype(BF16)


def _group_norm64(x, g):
    bd = _block_diag_ones(NORM_SLAB, HEAD_DIM)
    outs = []
    for j in range(x.shape[-1] // NORM_SLAB):
        xs = x[:, j * NORM_SLAB:(j + 1) * NORM_SLAB]
        sq = xs * xs
        hi = sq.astype(BF16)
        lo = (sq - hi.astype(F32)).astype(BF16)
        ssq = _dot(hi, bd) + _dot(lo, bd)
        outs.append(xs * lax.rsqrt(ssq * (1.0 / HEAD_DIM) + EPS))
    y = outs[0] if len(outs) == 1 else jnp.concatenate(outs, axis=-1)
    return y * g


def _gelu_exact(x):
    return 0.5 * x * (1.0 + lax.erf(x * math.sqrt(0.5)))


def _ffn_kernel(x_ref, g_ref, win_ref, wout_ref, o_ref):
    d_ff = wout_ref.shape[0]
    x = x_ref[...]
    h = _rms(x, g_ref[...]).astype(BF16)
    acc = jnp.zeros(x.shape, F32)
    for c in range(d_ff // FFN_CK):
        lo = c * FFN_CK
        gate = _dot(h, win_ref[:, lo:lo + FFN_CK])
        up = _dot(h, win_ref[:, d_ff + lo:d_ff + lo + FFN_CK])
        a = (gate * jax.nn.sigmoid(gate) * up).astype(BF16)
        acc = acc + _dot(a, wout_ref[lo:lo + FFN_CK, :])
    o_ref[...] = x + 0.5 * acc


def _ffn(x2d, g, w_in, w_out):
    n, d = x2d.shape
    d_ff = w_out.shape[0]
    return pl.pallas_call(
        _ffn_kernel,
        grid=(n // FFN_TM,),
        in_specs=[
            pl.BlockSpec((FFN_TM, d), lambda i: (i, 0)),
            _const_spec((1, d)),
            _const_spec((d, 2 * d_ff)),
            _const_spec((d_ff, d)),
        ],
        out_specs=pl.BlockSpec((FFN_TM, d), lambda i: (i, 0)),
        out_shape=jax.ShapeDtypeStruct((n, d), F32),
        compiler_params=pltpu.CompilerParams(
            dimension_semantics=("arbitrary",), vmem_limit_bytes=VMEM_LIMIT),
        name="ffn",
    )(x2d, g, w_in, w_out)


def _memkv_kernel(mem_ref, g_ref, w_ref, gk_ref, k_ref, v_ref):
    mh = _rms(mem_ref[0], g_ref[0]).astype(BF16)
    kv = _dot(mh, w_ref[0])
    k = _group_norm64(kv[:, :MEM_W], gk_ref[0]) * ATTN_SCALE
    k_ref[0, 0] = k.astype(BF16)
    v_ref[0, 0] = kv[:, MEM_W:].astype(BF16)


def _memkv(mem, mem_norm, w_kv, gk_tiled):
    depth = w_kv.shape[0]
    b, l, d = mem.shape
    out = jax.ShapeDtypeStruct((depth, b, l, MEM_W), BF16)
    return pl.pallas_call(
        _memkv_kernel,
        grid=(depth, b),
        in_specs=[
            pl.BlockSpec((1, l, d), lambda i, j: (j, 0, 0)),
            pl.BlockSpec((1, 1, d), lambda i, j: (i, 0, 0)),
            pl.BlockSpec((1, d, 2 * MEM_W), lambda i, j: (i, 0, 0)),
            pl.BlockSpec((1, 1, MEM_W), lambda i, j: (i, 0, 0)),
        ],
        out_specs=[
            pl.BlockSpec((1, 1, l, MEM_W), lambda i, j: (i, j, 0, 0)),
            pl.BlockSpec((1, 1, l, MEM_W), lambda i, j: (i, j, 0, 0)),
        ],
        out_shape=[out, out],
        compiler_params=pltpu.CompilerParams(
            dimension_semantics=("arbitrary", "arbitrary"), vmem_limit_bytes=VMEM_LIMIT),
        name="memkv",
    )(mem, mem_norm, w_kv, gk_tiled)


def _mem_attention(qn, k, v):
    lane = lax.broadcasted_iota(jnp.int32, (1, MEM_W), 1)
    out = jnp.zeros((qn.shape[0], MEM_W), F32)
    for h in range(MEM_HEADS):
        sel = (lane >= h * HEAD_DIM) & (lane < (h + 1) * HEAD_DIM)
        qh = jnp.where(sel, qn, jnp.zeros_like(qn))
        s = _dot_nt(qh, k)
        p = jnp.exp(s - jnp.max(s, axis=-1, keepdims=True))
        l = jnp.sum(p, axis=-1, keepdims=True)
        vh = jnp.where(sel, v, jnp.zeros_like(v))
        out = out + _dot(p.astype(BF16), vh) / l
    return out


def _gmlp_mix_kernel(x_ref, g_ref, win_ref, vg_ref, ws_ref, bst_ref, gq_ref,
                     k_ref, v_ref, wout_ref, o_ref):
    x = x_ref[...]
    h = _rms(x, g_ref[...]).astype(BF16)
    z = _dot(h, win_ref[...])
    u = _gelu_exact(z[:, :TOK_W])
    vact = _gelu_exact(z[:, TOK_W:2 * TOK_W])
    qn = _group_norm64(z[:, 2 * TOK_W:], gq_ref[...]).astype(BF16)

    row = lax.broadcasted_iota(jnp.int32, (CHUNK, CHUNK), 0)
    col = lax.broadcasted_iota(jnp.int32, (CHUNK, CHUNK), 1)
    causal = col <= row
    bst = bst_ref[...]

    tok_cols = []
    for g in range(GROUPS):
        sl = slice(g * GROUP_DIM, (g + 1) * GROUP_DIM)
        vn = _rms(vact[:, sl], vg_ref[:, sl]).astype(BF16)
        ws = jnp.where(causal, ws_ref[g], 0.0).astype(BF16)
        bias = bst[:, g:g + 1]
        rows = []
        for n in range(x.shape[0] // CHUNK):
            rs = slice(n * CHUNK, (n + 1) * CHUNK)
            mixed = _dot(ws, vn[rs]) + bias
            rows.append(u[rs, sl] * mixed)
        tok_cols.append(jnp.concatenate(rows, axis=0))
    tok = jnp.concatenate(tok_cols, axis=-1).astype(BF16)

    mo = _mem_attention(qn, k_ref[0], v_ref[0]).astype(BF16)
    y = _dot(tok, wout_ref[:TOK_W, :]) + _dot(mo, wout_ref[TOK_W:, :])
    o_ref[...] = x + y


def _gmlp_mix(x2d, seq, g, w_in, v_gain, w_s, b_st, gq_tiled, kmem, vmem, w_out):
    n, d = x2d.shape
    tiles_per_row = seq // MIX_TM
    l = kmem.shape[1]
    return pl.pallas_call(
        _gmlp_mix_kernel,
        grid=(n // MIX_TM,),
        in_specs=[
            pl.BlockSpec((MIX_TM, d), lambda i: (i, 0)),
            _const_spec((1, d)),
            _const_spec(w_in.shape),
            _const_spec((1, TOK_W)),
            _const_spec(w_s.shape),
            _const_spec(b_st.shape),
            _const_spec((1, MEM_W)),
            pl.BlockSpec((1, l, MEM_W), lambda i: (i // tiles_per_row, 0, 0)),
            pl.BlockSpec((1, l, MEM_W), lambda i: (i // tiles_per_row, 0, 0)),
            _const_spec(w_out.shape),
        ],
        out_specs=pl.BlockSpec((MIX_TM, d), lambda i: (i, 0)),
        out_shape=jax.ShapeDtypeStruct((n, d), F32),
        compiler_params=pltpu.CompilerParams(
            dimension_semantics=("arbitrary",), vmem_limit_bytes=VMEM_LIMIT),
        name="gmlp_mix",
    )(x2d, g, w_in, v_gain, w_s, b_st, gq_tiled, kmem, vmem, w_out)


def _diff_inproj_kernel(x_ref, g_ref, win_ref, gq_ref, gk_ref, gqm_ref,
                        q_ref, k_ref, v_ref, qm_ref):
    h = _rms(x_ref[...], g_ref[...]).astype(BF16)
    z = _dot(h, win_ref[...])
    q = _group_norm64(z[:, :TOK_W], gq_ref[...]) * ATTN_SCALE
    k = _group_norm64(z[:, TOK_W:2 * TOK_W], gk_ref[...])
    q_ref[...] = q.astype(BF16)
    k_ref[...] = k.astype(BF16)
    v_ref[...] = z[:, 2 * TOK_W:3 * TOK_W].astype(BF16)
    qm_ref[...] = _group_norm64(z[:, 3 * TOK_W:], gqm_ref[...]).astype(BF16)


def _diff_inproj(x2d, g, w_in, gq_tiled, gk_tiled, gqm_tiled):
    n, d = x2d.shape
    tok_spec = pl.BlockSpec((MIX_TM, TOK_W), lambda i: (i, 0))
    tok_shape = jax.ShapeDtypeStruct((n, TOK_W), BF16)
    return pl.pallas_call(
        _diff_inproj_kernel,
        grid=(n // MIX_TM,),
        in_specs=[
            pl.BlockSpec((MIX_TM, d), lambda i: (i, 0)),
            _const_spec((1, d)),
            _const_spec(w_in.shape),
            _const_spec((1, TOK_W)),
            _const_spec((1, TOK_W)),
            _const_spec((1, MEM_W)),
        ],
        out_specs=[tok_spec, tok_spec, tok_spec,
                   pl.BlockSpec((MIX_TM, MEM_W), lambda i: (i, 0))],
        out_shape=[tok_shape, tok_shape, tok_shape,
                   jax.ShapeDtypeStruct((n, MEM_W), BF16)],
        compiler_params=pltpu.CompilerParams(
            dimension_semantics=("arbitrary",), vmem_limit_bytes=VMEM_LIMIT),
        name="diff_inproj",
    )(x2d, g, w_in, gq_tiled, gk_tiled, gqm_tiled)


def _diff_attn_kernel(lam_ref, sg_ref, q_ref, k_ref, v_ref, o_ref, *, lambda_init):
    qi = pl.program_id(2)
    q = q_ref[0]
    lane = lax.broadcasted_iota(jnp.int32, (1, VAL_DIM), 1)
    zero = jnp.zeros_like(q)
    qq = jnp.concatenate([jnp.where(lane < HEAD_DIM, q, zero),
                          jnp.where(lane >= HEAD_DIM, q, zero)], axis=0)

    def scores(j):
        kb = k_ref[0, pl.ds(pl.multiple_of(j * ATTN_TK, ATTN_TK), ATTN_TK), :]
        vb = v_ref[0, pl.ds(pl.multiple_of(j * ATTN_TK, ATTN_TK), ATTN_TK), :]
        return _dot_nt(qq, kb), vb

    s, vb = scores(qi)
    r = lax.broadcasted_iota(jnp.int32, (ATTN_TQ, ATTN_TK), 0)
    c = lax.broadcasted_iota(jnp.int32, (ATTN_TQ, ATTN_TK), 1)
    keep = jnp.concatenate([c <= r, c <= r], axis=0)
    s = jnp.where(keep, s, -jnp.inf)
    m = jnp.max(s, axis=-1, keepdims=True)
    p = jnp.exp(s - m)
    l = jnp.sum(p, axis=-1, keepdims=True)
    acc = _dot(p.astype(BF16), vb)

    def body(j, carry):
        m, l, acc = carry
        s, vb = scores(j)
        m_new = jnp.maximum(m, jnp.max(s, axis=-1, keepdims=True))
        alpha = jnp.exp(m - m_new)
        p = jnp.exp(s - m_new)
        l = alpha * l + jnp.sum(p, axis=-1, keepdims=True)
        acc = alpha * acc + _dot(p.astype(BF16), vb)
        return m_new, l, acc

    m, l, acc = lax.fori_loop(0, qi, body, (m, l, acc))
    o2 = acc / l
    lp = lam_ref[...]
    lam = (jnp.exp(jnp.sum(lp[0:1] * lp[1:2], axis=-1, keepdims=True))
           - jnp.exp(jnp.sum(lp[2:3] * lp[3:4], axis=-1, keepdims=True)) + lambda_init)
    o = o2[:ATTN_TQ] - lam * o2[ATTN_TQ:]
    o = _rms(o, sg_ref[...]) * (1.0 - lambda_init)
    o_ref[0] = o.astype(BF16)


def _diff_attn(q, k, v, lam_p, subln_g, lambda_init):
    b, s, _ = q.shape
    kernel = functools.partial(_diff_attn_kernel, lambda_init=lambda_init)
    return pl.pallas_call(
        kernel,
        grid=(b, DIFF_HEADS, s // ATTN_TQ),
        in_specs=[
            _const_spec(lam_p.shape),
            _const_spec((1, VAL_DIM)),
            pl.BlockSpec((1, ATTN_TQ, VAL_DIM), lambda bi, h, i: (bi, i, h)),
            pl.BlockSpec((1, s, VAL_DIM), lambda bi, h, i: (bi, 0, h)),
            pl.BlockSpec((1, s, VAL_DIM), lambda bi, h, i: (bi, 0, h)),
        ],
        out_specs=pl.BlockSpec((1, ATTN_TQ, VAL_DIM), lambda bi, h, i: (bi, i, h)),
        out_shape=jax.ShapeDtypeStruct((b, s, TOK_W), BF16),
        compiler_params=pltpu.CompilerParams(
            dimension_semantics=("arbitrary", "arbitrary", "arbitrary"),
            vmem_limit_bytes=VMEM_LIMIT),
        name="diff_attn",
    )(lam_p, subln_g, q, k, v)


def _diff_outproj_kernel(x_ref, tok_ref, qm_ref, k_ref, v_ref, wout_ref, o_ref):
    mo = _mem_attention(qm_ref[...], k_ref[0], v_ref[0]).astype(BF16)
    y = _dot(tok_ref[...], wout_ref[:TOK_W, :]) + _dot(mo, wout_ref[TOK_W:, :])
    o_ref[...] = x_ref[...] + y


def _diff_outproj(x2d, seq, tok, qm, kmem, vmem, w_out):
    n, d = x2d.shape
    tiles_per_row = seq // MIX_TM
    l = kmem.shape[1]
    return pl.pallas_call(
        _diff_outproj_kernel,
        grid=(n // MIX_TM,),
        in_specs=[
            pl.BlockSpec((MIX_TM, d), lambda i: (i, 0)),
            pl.BlockSpec((MIX_TM, TOK_W), lambda i: (i, 0)),
            pl.BlockSpec((MIX_TM, MEM_W), lambda i: (i, 0)),
            pl.BlockSpec((1, l, MEM_W), lambda i: (i // tiles_per_row, 0, 0)),
            pl.BlockSpec((1, l, MEM_W), lambda i: (i // tiles_per_row, 0, 0)),
            _const_spec(w_out.shape),
        ],
        out_specs=pl.BlockSpec((MIX_TM, d), lambda i: (i, 0)),
        out_shape=jax.ShapeDtypeStruct((n, d), F32),
        compiler_params=pltpu.CompilerParams(
            dimension_semantics=("arbitrary",), vmem_limit_bytes=VMEM_LIMIT),
        name="diff_outproj",
    )(x2d, tok, qm, kmem, vmem, w_out)


def _tile_gain(g, reps):
    return jnp.tile(g.reshape(1, -1), (1, reps))


def kernel(x, mem, ffn_norm, ffn_w_in, ffn_w_out, mix_norm, mem_norm, w_mem_kv,
           memq_norm, memk_norm, w_out, a_w_in, a_v_norm, a_w_s, a_b_s,
           b_w_in, b_q_norm, b_k_norm, b_lambda, b_subln):
    batch, seq, d = x.shape
    depth = ffn_norm.shape[0]
    x2d = x.reshape(batch * seq, d)

    gk_mem = jnp.tile(memk_norm.reshape(depth, 1, HEAD_DIM), (1, 1, MEM_HEADS))
    kmem, vmem = _memkv(mem, mem_norm.reshape(depth, 1, d), w_mem_kv.astype(BF16), gk_mem)

    for i in range(depth):
        j = i // N_MIXERS
        x2d = _ffn(x2d, ffn_norm[i, 0].reshape(1, d),
                   ffn_w_in[i, 0].astype(BF16), ffn_w_out[i, 0].astype(BF16))
        g_mix = mix_norm[i].reshape(1, d)
        gq_mem = _tile_gain(memq_norm[i], MEM_HEADS)
        w_o = w_out[i].astype(BF16)
        if i % N_MIXERS == 0:
            x2d = _gmlp_mix(x2d, seq, g_mix, a_w_in[j].astype(BF16),
                            a_v_norm[j].reshape(1, TOK_W), a_w_s[j], a_b_s[j].T,
                            gq_mem, kmem[i], vmem[i], w_o)
        else:
            lambda_init = 0.8 - 0.6 * math.exp(-0.3 * i)
            q, k, v, qm = _diff_inproj(
                x2d, g_mix, b_w_in[j].astype(BF16),
                _tile_gain(b_q_norm[j], 2 * DIFF_HEADS),
                _tile_gain(b_k_norm[j], 2 * DIFF_HEADS), gq_mem)
            tok = _diff_attn(q.reshape(batch, seq, TOK_W), k.reshape(batch, seq, TOK_W),
                             v.reshape(batch, seq, TOK_W), b_lambda[j],
                             b_subln[j].reshape(1, VAL_DIM), lambda_init)
            x2d = _diff_outproj(x2d, seq, tok.reshape(batch * seq, TOK_W), qm,
                                kmem[i], vmem[i], w_o)
        x2d = _ffn(x2d, ffn_norm[i, 1].reshape(1, d),
                   ffn_w_in[i, 1].astype(BF16), ffn_w_out[i, 1].astype(BF16))
    return x2d.reshape(batch, seq, d)
```

```python
import functools
import math

import jax
import jax.numpy as jnp
from jax import lax
from jax.experimental import pallas as pl
from jax.experimental.pallas import tpu as pltpu

D_MODEL = 1024
HEAD_DIM = 64
MEM_HEADS = 4
MEM_W = MEM_HEADS * HEAD_DIM
TOK_W = D_MODEL - MEM_W
CHUNK = 128
GROUP_DIM = 128
GROUPS = TOK_W // GROUP_DIM
DIFF_HEADS = TOK_W // (2 * HEAD_DIM)
VAL_DIM = 2 * HEAD_DIM
N_MIXERS = 2
EPS = 1e-6
ATTN_SCALE = HEAD_DIM ** -0.5
LOG2E = math.log2(math.e)
FAST_MAX_LOG2 = 60.0

LANES = 128
NORM_SLAB = 256

FFN_TM = 256
FFN_CK = 256
MIX_TM = 256
ATTN_TQ = 256
ATTN_TK = 256
VMEM_LIMIT = 56 * 1024 * 1024

F32 = jnp.float32
BF16 = jnp.bfloat16


def _const_spec(shape):
    nd = len(shape)
    return pl.BlockSpec(shape, lambda *_: (0,) * nd, pipeline_mode=pl.Buffered(1))


def _rms(x, g):
    ms = jnp.mean(x * x, axis=-1, keepdims=True)
    return x * lax.rsqrt(ms + EPS) * g


def _dot(a, b):
    return jnp.dot(a, b, preferred_element_type=F32)


def _dot_nt(a, b):
    return lax.dot_general(a, b, (((1,), (1,)), ((), ())), preferred_element_type=F32)


def _block_diag_ones(width, group):
    r = lax.broadcasted_iota(jnp.int32, (width, width), 0) // group
    c = lax.broadcasted_iota(jnp.int32, (width, width), 1) // group
    return jnp.where(r == c, 1.0, 0.0).astype(BF16)


def _group_norm64(x, g):
    bd = _block_diag_ones(NORM_SLAB, HEAD_DIM)
    outs = []
    for j in range(x.shape[-1] // NORM_SLAB):
        xs = x[:, j * NORM_SLAB:(j + 1) * NORM_SLAB]
        sq = xs * xs
        hi = sq.astype(BF16)
        lo = (sq - hi.astype(F32)).astype(BF16)
        ssq = _dot(hi, bd) + _dot(lo, bd)
        outs.append(xs * lax.rsqrt(ssq * (1.0 / HEAD_DIM) + EPS))
    y = outs[0] if len(outs) == 1 else jnp.concatenate(outs, axis=-1)
    return y * g


def _gelu_exact(x):
    return 0.5 * x * (1.0 + lax.erf(x * math.sqrt(0.5)))


def _ffn_kernel(x_ref, g_ref, win_ref, wout_ref, o_ref):
    d_ff = wout_ref.shape[0]
    x = x_ref[...]
    h = _rms(x, g_ref[...]).astype(BF16)
    acc = jnp.zeros(x.shape, F32)
    for c in range(d_ff // FFN_CK):
        lo = c * FFN_CK
        gate = _dot(h, win_ref[:, lo:lo + FFN_CK])
        up = _dot(h, win_ref[:, d_ff + lo:d_ff + lo + FFN_CK])
        a = (gate * jax.nn.sigmoid(gate) * up).astype(BF16)
        acc = acc + _dot(a, wout_ref[lo:lo + FFN_CK, :])
    o_ref[...] = x + 0.5 * acc


def _ffn(x2d, g, w_in, w_out):
    n, d = x2d.shape
    d_ff = w_out.shape[0]
    return pl.pallas_call(
        _ffn_kernel,
        grid=(n // FFN_TM,),
        in_specs=[
            pl.BlockSpec((FFN_TM, d), lambda i: (i, 0)),
            _const_spec((1, d)),
            _const_spec((d, 2 * d_ff)),
            _const_spec((d_ff, d)),
        ],
        out_specs=pl.BlockSpec((FFN_TM, d), lambda i: (i, 0)),
        out_shape=jax.ShapeDtypeStruct((n, d), F32),
        compiler_params=pltpu.CompilerParams(
            dimension_semantics=("arbitrary",), vmem_limit_bytes=VMEM_LIMIT),
        name="ffn",
    )(x2d, g, w_in, w_out)


def _memkv_kernel(mem_ref, g_ref, w_ref, gk_ref, k_ref, v_ref):
    mh = _rms(mem_ref[0], g_ref[0]).astype(BF16)
    kv = _dot(mh, w_ref[0])
    k = _group_norm64(kv[:, :MEM_W], gk_ref[0]) * ATTN_SCALE
    k_ref[0, 0] = k.astype(BF16)
    v_ref[0, 0] = kv[:, MEM_W:].astype(BF16)


def _memkv(mem, mem_norm, w_kv, gk_tiled):
    depth = w_kv.shape[0]
    b, l, d = mem.shape
    out = jax.ShapeDtypeStruct((depth, b, l, MEM_W), BF16)
    return pl.pallas_call(
        _memkv_kernel,
        grid=(depth, b),
        in_specs=[
            pl.BlockSpec((1, l, d), lambda i, j: (j, 0, 0)),
            pl.BlockSpec((1, 1, d), lambda i, j: (i, 0, 0)),
            pl.BlockSpec((1, d, 2 * MEM_W), lambda i, j: (i, 0, 0)),
            pl.BlockSpec((1, 1, MEM_W), lambda i, j: (i, 0, 0)),
        ],
        out_specs=[
            pl.BlockSpec((1, 1, l, MEM_W), lambda i, j: (i, j, 0, 0)),
            pl.BlockSpec((1, 1, l, MEM_W), lambda i, j: (i, j, 0, 0)),
        ],
        out_shape=[out, out],
        compiler_params=pltpu.CompilerParams(
            dimension_semantics=("arbitrary", "arbitrary"), vmem_limit_bytes=VMEM_LIMIT),
        name="memkv",
    )(mem, mem_norm, w_kv, gk_tiled)


def _mem_attention(qn, k, v):
    lane = lax.broadcasted_iota(jnp.int32, (1, MEM_W), 1)
    out = jnp.zeros((qn.shape[0], MEM_W), F32)
    for h in range(MEM_HEADS):
        sel = (lane >= h * HEAD_DIM) & (lane < (h + 1) * HEAD_DIM)
        qh = jnp.where(sel, qn, jnp.zeros_like(qn))
        s = _dot_nt(qh, k)
        p = jnp.exp(s - jnp.max(s, axis=-1, keepdims=True))
        l = jnp.sum(p, axis=-1, keepdims=True)
        vh = jnp.where(sel, v, jnp.zeros_like(v))
        out = out + _dot(p.astype(BF16), vh) / l
    return out


def _gmlp_mix_kernel(x_ref, g_ref, win_ref, vg_ref, ws_ref, bst_ref, gq_ref,
                     k_ref, v_ref, wout_ref, o_ref):
    x = x_ref[...]
    h = _rms(x, g_ref[...]).astype(BF16)
    z = _dot(h, win_ref[...])
    u = _gelu_exact(z[:, :TOK_W])
    vact = _gelu_exact(z[:, TOK_W:2 * TOK_W])
    qn = _group_norm64(z[:, 2 * TOK_W:], gq_ref[...]).astype(BF16)

    row = lax.broadcasted_iota(jnp.int32, (CHUNK, CHUNK), 0)
    col = lax.broadcasted_iota(jnp.int32, (CHUNK, CHUNK), 1)
    causal = col <= row
    bst = bst_ref[...]

    tok_cols = []
    for g in range(GROUPS):
        sl = slice(g * GROUP_DIM, (g + 1) * GROUP_DIM)
        vn = _rms(vact[:, sl], vg_ref[:, sl]).astype(BF16)
        ws = jnp.where(causal, ws_ref[g], 0.0).astype(BF16)
        bias = bst[:, g:g + 1]
        rows = []
        for n in range(x.shape[0] // CHUNK):
            rs = slice(n * CHUNK, (n + 1) * CHUNK)
            mixed = _dot(ws, vn[rs]) + bias
            rows.append(u[rs, sl] * mixed)
        tok_cols.append(jnp.concatenate(rows, axis=0))
    tok = jnp.concatenate(tok_cols, axis=-1).astype(BF16)

    mo = _mem_attention(qn, k_ref[0], v_ref[0]).astype(BF16)
    y = _dot(tok, wout_ref[:TOK_W, :]) + _dot(mo, wout_ref[TOK_W:, :])
    o_ref[...] = x + y


def _gmlp_mix(x2d, seq, g, w_in, v_gain, w_s, b_st, gq_tiled, kmem, vmem, w_out):
    n, d = x2d.shape
    tiles_per_row = seq // MIX_TM
    l = kmem.shape[1]
    return pl.pallas_call(
        _gmlp_mix_kernel,
        grid=(n // MIX_TM,),
        in_specs=[
            pl.BlockSpec((MIX_TM, d), lambda i: (i, 0)),
            _const_spec((1, d)),
            _const_spec(w_in.shape),
            _const_spec((1, TOK_W)),
            _const_spec(w_s.shape),
            _const_spec(b_st.shape),
            _const_spec((1, MEM_W)),
            pl.BlockSpec((1, l, MEM_W), lambda i: (i // tiles_per_row, 0, 0)),
            pl.BlockSpec((1, l, MEM_W), lambda i: (i // tiles_per_row, 0, 0)),
            _const_spec(w_out.shape),
        ],
        out_specs=pl.BlockSpec((MIX_TM, d), lambda i: (i, 0)),
        out_shape=jax.ShapeDtypeStruct((n, d), F32),
        compiler_params=pltpu.CompilerParams(
            dimension_semantics=("arbitrary",), vmem_limit_bytes=VMEM_LIMIT),
        name="gmlp_mix",
    )(x2d, g, w_in, v_gain, w_s, b_st, gq_tiled, kmem, vmem, w_out)


def _diff_inproj_kernel(x_ref, g_ref, win_ref, gq_ref, gk_ref, gqm_ref,
                        q_ref, k_ref, v_ref, qm_ref):
    h = _rms(x_ref[...], g_ref[...]).astype(BF16)
    z = _dot(h, win_ref[...])
    q = _group_norm64(z[:, :TOK_W], gq_ref[...]) * (ATTN_SCALE * LOG2E)
    k = _group_norm64(z[:, TOK_W:2 * TOK_W], gk_ref[...])
    q_ref[...] = q.astype(BF16)
    k_ref[...] = k.astype(BF16)
    v_ref[...] = z[:, 2 * TOK_W:3 * TOK_W].astype(BF16)
    qm_ref[...] = _group_norm64(z[:, 3 * TOK_W:], gqm_ref[...]).astype(BF16)


def _diff_inproj(x2d, g, w_in, gq_tiled, gk_tiled, gqm_tiled):
    n, d = x2d.shape
    tok_spec = pl.BlockSpec((MIX_TM, TOK_W), lambda i: (i, 0))
    tok_shape = jax.ShapeDtypeStruct((n, TOK_W), BF16)
    return pl.pallas_call(
        _diff_inproj_kernel,
        grid=(n // MIX_TM,),
        in_specs=[
            pl.BlockSpec((MIX_TM, d), lambda i: (i, 0)),
            _const_spec((1, d)),
            _const_spec(w_in.shape),
            _const_spec((1, TOK_W)),
            _const_spec((1, TOK_W)),
            _const_spec((1, MEM_W)),
        ],
        out_specs=[tok_spec, tok_spec, tok_spec,
                   pl.BlockSpec((MIX_TM, MEM_W), lambda i: (i, 0))],
        out_shape=[tok_shape, tok_shape, tok_shape,
                   jax.ShapeDtypeStruct((n, MEM_W), BF16)],
        compiler_params=pltpu.CompilerParams(
            dimension_semantics=("arbitrary",), vmem_limit_bytes=VMEM_LIMIT),
        name="diff_inproj",
    )(x2d, g, w_in, gq_tiled, gk_tiled, gqm_tiled)


def _split_components(q):
    lane = lax.broadcasted_iota(jnp.int32, (1, VAL_DIM), 1)
    zero = jnp.zeros_like(q)
    return jnp.concatenate([jnp.where(lane < HEAD_DIM, q, zero),
                            jnp.where(lane >= HEAD_DIM, q, zero)], axis=0)


def _causal_keep(rows, cols):
    r = lax.broadcasted_iota(jnp.int32, (rows, cols), 0)
    c = lax.broadcasted_iota(jnp.int32, (rows, cols), 1)
    return jnp.concatenate([c <= r, c <= r], axis=0)


def _diff_combine(o2, lam_ref, sg_ref, lambda_init):
    rows = o2.shape[0] // 2
    lp = lam_ref[...]
    lam = (jnp.exp(jnp.sum(lp[0:1] * lp[1:2], axis=-1, keepdims=True))
           - jnp.exp(jnp.sum(lp[2:3] * lp[3:4], axis=-1, keepdims=True)) + lambda_init)
    o = o2[:rows] - lam * o2[rows:]
    return (_rms(o, sg_ref[...]) * (1.0 - lambda_init)).astype(BF16)


def _diff_attn_online_kernel(lam_ref, sg_ref, q_ref, k_ref, v_ref, o_ref, *, lambda_init):
    qi = pl.program_id(2)
    qq = _split_components(q_ref[0])

    def scores(j):
        rows = pl.ds(pl.multiple_of(j * ATTN_TK, ATTN_TK), ATTN_TK)
        return _dot_nt(qq, k_ref[0, rows, :]), v_ref[0, rows, :]

    s, vb = scores(qi)
    s = jnp.where(_causal_keep(ATTN_TQ, ATTN_TK), s, -jnp.inf)
    m = jnp.max(s, axis=-1, keepdims=True)
    p = jnp.exp2(s - m)
    l = jnp.sum(p, axis=-1, keepdims=True)
    acc = _dot(p.astype(BF16), vb)

    def body(j, carry):
        m, l, acc = carry
        s, vb = scores(j)
        m_new = jnp.maximum(m, jnp.max(s, axis=-1, keepdims=True))
        alpha = jnp.exp2(m - m_new)
        p = jnp.exp2(s - m_new)
        l = alpha * l + jnp.sum(p, axis=-1, keepdims=True)
        acc = alpha * acc + _dot(p.astype(BF16), vb)
        return m_new, l, acc

    m, l, acc = lax.fori_loop(0, qi, body, (m, l, acc))
    o_ref[0] = _diff_combine(acc / l, lam_ref, sg_ref, lambda_init)


def _diff_attn_bounded_kernel(lam_ref, sg_ref, q_ref, k_ref, v_ref, o_ref, vext_ref,
                              *, lambda_init):
    s_len = q_ref.shape[1]
    lane = lax.broadcasted_iota(jnp.int32, (s_len, VAL_DIM), 1)
    vext_ref[:, :VAL_DIM] = v_ref[0]
    vext_ref[:, VAL_DIM:] = jnp.where(lane == 0, 1.0, 0.0).astype(BF16)
    keep = _causal_keep(ATTN_TQ, ATTN_TQ)

    for i in range(s_len // ATTN_TQ):
        lo = i * ATTN_TQ
        qq = _split_components(q_ref[0, lo:lo + ATTN_TQ, :])
        t = jnp.where(keep, _dot_nt(qq, k_ref[0, lo:lo + ATTN_TQ, :]), -jnp.inf)
        acc = _dot(jnp.exp2(t).astype(BF16), vext_ref[lo:lo + ATTN_TQ, :])
        if i > 0:
            t = _dot_nt(qq, k_ref[0, :lo, :])
            acc = acc + _dot(jnp.exp2(t).astype(BF16), vext_ref[:lo, :])
        o2 = acc[:, :VAL_DIM] * (1.0 / acc[:, VAL_DIM:VAL_DIM + 1])
        o_ref[0, lo:lo + ATTN_TQ, :] = _diff_combine(o2, lam_ref, sg_ref, lambda_init)


def _diff_attn(q, k, v, lam_p, subln_g, lambda_init, logit_bound):
    b, s, _ = q.shape
    out_shape = jax.ShapeDtypeStruct((b, s, TOK_W), BF16)

    def bounded(q, k, v):
        head = pl.BlockSpec((1, s, VAL_DIM), lambda bi, h: (bi, 0, h))
        return pl.pallas_call(
            functools.partial(_diff_attn_bounded_kernel, lambda_init=lambda_init),
            grid=(b, DIFF_HEADS),
            in_specs=[_const_spec(lam_p.shape), _const_spec((1, VAL_DIM)), head, head, head],
            out_specs=head,
            out_shape=out_shape,
            scratch_shapes=[pltpu.VMEM((s, 2 * VAL_DIM), BF16)],
            compiler_params=pltpu.CompilerParams(
                dimension_semantics=("arbitrary", "arbitrary"), vmem_limit_bytes=VMEM_LIMIT),
            name="diff_attn_bounded",
        )(lam_p, subln_g, q, k, v)

    def online(q, k, v):
        head = pl.BlockSpec((1, s, VAL_DIM), lambda bi, h, i: (bi, 0, h))
        blk = pl.BlockSpec((1, ATTN_TQ, VAL_DIM), lambda bi, h, i: (bi, i, h))
        return pl.pallas_call(
            functools.partial(_diff_attn_online_kernel, lambda_init=lambda_init),
            grid=(b, DIFF_HEADS, s // ATTN_TQ),
            in_specs=[_const_spec(lam_p.shape), _const_spec((1, VAL_DIM)), blk, head, head],
            out_specs=blk,
            out_shape=out_shape,
            compiler_params=pltpu.CompilerParams(
                dimension_semantics=("arbitrary", "arbitrary", "arbitrary"),
                vmem_limit_bytes=VMEM_LIMIT),
            name="diff_attn_online",
        )(lam_p, subln_g, q, k, v)

    return lax.cond(logit_bound <= FAST_MAX_LOG2, bounded, online, q, k, v)


def _diff_outproj_kernel(x_ref, tok_ref, qm_ref, k_ref, v_ref, wout_ref, o_ref):
    mo = _mem_attention(qm_ref[...], k_ref[0], v_ref[0]).astype(BF16)
    y = _dot(tok_ref[...], wout_ref[:TOK_W, :]) + _dot(mo, wout_ref[TOK_W:, :])
    o_ref[...] = x_ref[...] + y


def _diff_outproj(x2d, seq, tok, qm, kmem, vmem, w_out):
    n, d = x2d.shape
    tiles_per_row = seq // MIX_TM
    l = kmem.shape[1]
    return pl.pallas_call(
        _diff_outproj_kernel,
        grid=(n // MIX_TM,),
        in_specs=[
            pl.BlockSpec((MIX_TM, d), lambda i: (i, 0)),
            pl.BlockSpec((MIX_TM, TOK_W), lambda i: (i, 0)),
            pl.BlockSpec((MIX_TM, MEM_W), lambda i: (i, 0)),
            pl.BlockSpec((1, l, MEM_W), lambda i: (i // tiles_per_row, 0, 0)),
            pl.BlockSpec((1, l, MEM_W), lambda i: (i // tiles_per_row, 0, 0)),
            _const_spec(w_out.shape),
        ],
        out_specs=pl.BlockSpec((MIX_TM, d), lambda i: (i, 0)),
        out_shape=jax.ShapeDtypeStruct((n, d), F32),
        compiler_params=pltpu.CompilerParams(
            dimension_semantics=("arbitrary",), vmem_limit_bytes=VMEM_LIMIT),
        name="diff_outproj",
    )(x2d, tok, qm, kmem, vmem, w_out)


def _tile_gain(g, reps):
    return jnp.tile(g.reshape(1, -1), (1, reps))


def kernel(x, mem, ffn_norm, ffn_w_in, ffn_w_out, mix_norm, mem_norm, w_mem_kv,
           memq_norm, memk_norm, w_out, a_w_in, a_v_norm, a_w_s, a_b_s,
           b_w_in, b_q_norm, b_k_norm, b_lambda, b_subln):
    batch, seq, d = x.shape
    depth = ffn_norm.shape[0]
    x2d = x.reshape(batch * seq, d)

    gk_mem = jnp.tile(memk_norm.reshape(depth, 1, HEAD_DIM), (1, 1, MEM_HEADS))
    kmem, vmem = _memkv(mem, mem_norm.reshape(depth, 1, d), w_mem_kv.astype(BF16), gk_mem)

    for i in range(depth):
        j = i // N_MIXERS
        x2d = _ffn(x2d, ffn_norm[i, 0].reshape(1, d),
                   ffn_w_in[i, 0].astype(BF16), ffn_w_out[i, 0].astype(BF16))
        g_mix = mix_norm[i].reshape(1, d)
        gq_mem = _tile_gain(memq_norm[i], MEM_HEADS)
        w_o = w_out[i].astype(BF16)
        if i % N_MIXERS == 0:
            x2d = _gmlp_mix(x2d, seq, g_mix, a_w_in[j].astype(BF16),
                            a_v_norm[j].reshape(1, TOK_W), a_w_s[j], a_b_s[j].T,
                            gq_mem, kmem[i], vmem[i], w_o)
        else:
            lambda_init = 0.8 - 0.6 * math.exp(-0.3 * i)
            q, k, v, qm = _diff_inproj(
                x2d, g_mix, b_w_in[j].astype(BF16),
                _tile_gain(b_q_norm[j], 2 * DIFF_HEADS),
                _tile_gain(b_k_norm[j], 2 * DIFF_HEADS), gq_mem)
            logit_bound = (1.01 * HEAD_DIM * ATTN_SCALE * LOG2E
                           * jnp.max(jnp.abs(b_q_norm[j])) * jnp.max(jnp.abs(b_k_norm[j])))
            tok = _diff_attn(q.reshape(batch, seq, TOK_W), k.reshape(batch, seq, TOK_W),
                             v.reshape(batch, seq, TOK_W), b_lambda[j],
                             b_subln[j].reshape(1, VAL_DIM), lambda_init, logit_bound)
            x2d = _diff_outproj(x2d, seq, tok.reshape(batch * seq, TOK_W), qm,
                                kmem[i], vmem[i], w_o)
        x2d = _ffn(x2d, ffn_norm[i, 1].reshape(1, d),
                   ffn_w_in[i, 1].astype(BF16), ffn_w_out[i, 1].astype(BF16))
    return x2d.reshape(batch, seq, d)
```

```python
import functools
import math

import jax
import jax.numpy as jnp
from jax import lax
from jax.experimental import pallas as pl
from jax.experimental.pallas import tpu as pltpu

D_MODEL = 1024
HEAD_DIM = 64
MEM_HEADS = 4
MEM_W = MEM_HEADS * HEAD_DIM
TOK_W = D_MODEL - MEM_W
CHUNK = 128
GROUP_DIM = 128
GROUPS = TOK_W // GROUP_DIM
DIFF_HEADS = TOK_W // (2 * HEAD_DIM)
VAL_DIM = 2 * HEAD_DIM
N_MIXERS = 2
EPS = 1e-6
ATTN_SCALE = HEAD_DIM ** -0.5
LOG2E = math.log2(math.e)
FAST_MAX_LOG2 = 60.0

LANES = 128
NORM_SLAB = 256

FFN_TM = 512
FFN_CK = 2816
MIX_TM = 512
MIX_SUB = 256
ATTN_TQ = 256
ATTN_TK = 256
VMEM_LIMIT = 56 * 1024 * 1024

F32 = jnp.float32
BF16 = jnp.bfloat16


def _const_spec(shape):
    nd = len(shape)
    return pl.BlockSpec(shape, lambda *_: (0,) * nd, pipeline_mode=pl.Buffered(1))


def _rms(x, g):
    ms = jnp.mean(x * x, axis=-1, keepdims=True)
    return x * lax.rsqrt(ms + EPS) * g


def _dot(a, b):
    return jnp.dot(a, b, preferred_element_type=F32)


def _dot_nt(a, b):
    return lax.dot_general(a, b, (((1,), (1,)), ((), ())), preferred_element_type=F32)


def _block_diag_ones(width, group):
    r = lax.broadcasted_iota(jnp.int32, (width, width), 0) // group
    c = lax.broadcasted_iota(jnp.int32, (width, width), 1) // group
    return jnp.where(r == c, 1.0, 0.0).astype(BF16)


def _group_norm64(x, g):
    bd = _block_diag_ones(NORM_SLAB, HEAD_DIM)
    outs = []
    for j in range(x.shape[-1] // NORM_SLAB):
        xs = x[:, j * NORM_SLAB:(j + 1) * NORM_SLAB]
        sq = xs * xs
        hi = sq.astype(BF16)
        lo = (sq - hi.astype(F32)).astype(BF16)
        ssq = _dot(hi, bd) + _dot(lo, bd)
        outs.append(xs * lax.rsqrt(ssq * (1.0 / HEAD_DIM) + EPS))
    y = outs[0] if len(outs) == 1 else jnp.concatenate(outs, axis=-1)
    return y * g


def _gelu_exact(x):
    return 0.5 * x * (1.0 + lax.erf(x * math.sqrt(0.5)))


def _ffn_kernel(x_ref, g_ref, win_ref, wout_ref, o_ref):
    d_ff = wout_ref.shape[0]
    x = x_ref[...]
    h = _rms(x, g_ref[...]).astype(BF16)
    acc = None
    for lo in range(0, d_ff, FFN_CK):
        hi = min(lo + FFN_CK, d_ff)
        gate = _dot(h, win_ref[:, lo:hi])
        up = _dot(h, win_ref[:, d_ff + lo:d_ff + hi])
        a = (gate * jax.nn.sigmoid(gate) * up).astype(BF16)
        part = _dot(a, wout_ref[lo:hi, :])
        acc = part if acc is None else acc + part
    o_ref[...] = x + 0.5 * acc


def _ffn(x2d, g, w_in, w_out):
    n, d = x2d.shape
    d_ff = w_out.shape[0]
    return pl.pallas_call(
        _ffn_kernel,
        grid=(n // FFN_TM,),
        in_specs=[
            pl.BlockSpec((FFN_TM, d), lambda i: (i, 0)),
            _const_spec((1, d)),
            _const_spec((d, 2 * d_ff)),
            _const_spec((d_ff, d)),
        ],
        out_specs=pl.BlockSpec((FFN_TM, d), lambda i: (i, 0)),
        out_shape=jax.ShapeDtypeStruct((n, d), F32),
        compiler_params=pltpu.CompilerParams(
            dimension_semantics=("arbitrary",), vmem_limit_bytes=VMEM_LIMIT),
        name="ffn",
    )(x2d, g, w_in, w_out)


def _memkv_kernel(mem_ref, g_ref, w_ref, gk_ref, k_ref, v_ref):
    mh = _rms(mem_ref[0], g_ref[0]).astype(BF16)
    kv = _dot(mh, w_ref[0])
    k = _group_norm64(kv[:, :MEM_W], gk_ref[0]) * ATTN_SCALE
    k_ref[0, 0] = k.astype(BF16)
    v_ref[0, 0] = kv[:, MEM_W:].astype(BF16)


def _memkv(mem, mem_norm, w_kv, gk_tiled):
    depth = w_kv.shape[0]
    b, l, d = mem.shape
    out = jax.ShapeDtypeStruct((depth, b, l, MEM_W), BF16)
    return pl.pallas_call(
        _memkv_kernel,
        grid=(depth, b),
        in_specs=[
            pl.BlockSpec((1, l, d), lambda i, j: (j, 0, 0)),
            pl.BlockSpec((1, 1, d), lambda i, j: (i, 0, 0)),
            pl.BlockSpec((1, d, 2 * MEM_W), lambda i, j: (i, 0, 0)),
            pl.BlockSpec((1, 1, MEM_W), lambda i, j: (i, 0, 0)),
        ],
        out_specs=[
            pl.BlockSpec((1, 1, l, MEM_W), lambda i, j: (i, j, 0, 0)),
            pl.BlockSpec((1, 1, l, MEM_W), lambda i, j: (i, j, 0, 0)),
        ],
        out_shape=[out, out],
        compiler_params=pltpu.CompilerParams(
            dimension_semantics=("arbitrary", "arbitrary"), vmem_limit_bytes=VMEM_LIMIT),
        name="memkv",
    )(mem, mem_norm, w_kv, gk_tiled)


def _mem_attention(qn, k, v):
    lane = lax.broadcasted_iota(jnp.int32, (1, MEM_W), 1)
    out = jnp.zeros((qn.shape[0], MEM_W), F32)
    for h in range(MEM_HEADS):
        sel = (lane >= h * HEAD_DIM) & (lane < (h + 1) * HEAD_DIM)
        qh = jnp.where(sel, qn, jnp.zeros_like(qn))
        s = _dot_nt(qh, k)
        p = jnp.exp(s - jnp.max(s, axis=-1, keepdims=True))
        l = jnp.sum(p, axis=-1, keepdims=True)
        vh = jnp.where(sel, v, jnp.zeros_like(v))
        out = out + _dot(p.astype(BF16), vh) / l
    return out


def _gmlp_mix_kernel(x_ref, g_ref, win_ref, vg_ref, ws_ref, bst_ref, gq_ref,
                     k_ref, v_ref, wout_ref, o_ref):
    row = lax.broadcasted_iota(jnp.int32, (CHUNK, CHUNK), 0)
    col = lax.broadcasted_iota(jnp.int32, (CHUNK, CHUNK), 1)
    causal = col <= row
    ws = [jnp.where(causal, ws_ref[g], 0.0).astype(BF16) for g in range(GROUPS)]
    bst = bst_ref[...]

    for r0 in range(0, x_ref.shape[0], MIX_SUB):
        x = x_ref[r0:r0 + MIX_SUB, :]
        h = _rms(x, g_ref[...]).astype(BF16)
        z = _dot(h, win_ref[...])
        u = _gelu_exact(z[:, :TOK_W])
        vact = _gelu_exact(z[:, TOK_W:2 * TOK_W])
        qn = _group_norm64(z[:, 2 * TOK_W:], gq_ref[...]).astype(BF16)

        tok_cols = []
        for g in range(GROUPS):
            sl = slice(g * GROUP_DIM, (g + 1) * GROUP_DIM)
            vn = _rms(vact[:, sl], vg_ref[:, sl]).astype(BF16)
            bias = bst[:, g:g + 1]
            rows = []
            for n in range(MIX_SUB // CHUNK):
                rs = slice(n * CHUNK, (n + 1) * CHUNK)
                mixed = _dot(ws[g], vn[rs]) + bias
                rows.append(u[rs, sl] * mixed)
            tok_cols.append(jnp.concatenate(rows, axis=0))
        tok = jnp.concatenate(tok_cols, axis=-1).astype(BF16)

        mo = _mem_attention(qn, k_ref[0], v_ref[0]).astype(BF16)
        y = _dot(tok, wout_ref[:TOK_W, :]) + _dot(mo, wout_ref[TOK_W:, :])
        o_ref[r0:r0 + MIX_SUB, :] = x + y


def _gmlp_mix(x2d, seq, g, w_in, v_gain, w_s, b_st, gq_tiled, kmem, vmem, w_out):
    n, d = x2d.shape
    tiles_per_row = seq // MIX_TM
    l = kmem.shape[1]
    return pl.pallas_call(
        _gmlp_mix_kernel,
        grid=(n // MIX_TM,),
        in_specs=[
            pl.BlockSpec((MIX_TM, d), lambda i: (i, 0)),
            _const_spec((1, d)),
            _const_spec(w_in.shape),
            _const_spec((1, TOK_W)),
            _const_spec(w_s.shape),
            _const_spec(b_st.shape),
            _const_spec((1, MEM_W)),
            pl.BlockSpec((1, l, MEM_W), lambda i: (i // tiles_per_row, 0, 0)),
            pl.BlockSpec((1, l, MEM_W), lambda i: (i // tiles_per_row, 0, 0)),
            _const_spec(w_out.shape),
        ],
        out_specs=pl.BlockSpec((MIX_TM, d), lambda i: (i, 0)),
        out_shape=jax.ShapeDtypeStruct((n, d), F32),
        compiler_params=pltpu.CompilerParams(
            dimension_semantics=("arbitrary",), vmem_limit_bytes=VMEM_LIMIT),
        name="gmlp_mix",
    )(x2d, g, w_in, v_gain, w_s, b_st, gq_tiled, kmem, vmem, w_out)


def _diff_inproj_kernel(x_ref, g_ref, win_ref, gq_ref, gk_ref, gqm_ref,
                        q_ref, k_ref, v_ref, qm_ref):
    for r0 in range(0, x_ref.shape[0], MIX_SUB):
        rs = slice(r0, r0 + MIX_SUB)
        h = _rms(x_ref[rs, :], g_ref[...]).astype(BF16)
        z = _dot(h, win_ref[...])
        q = _group_norm64(z[:, :TOK_W], gq_ref[...]) * (ATTN_SCALE * LOG2E)
        k = _group_norm64(z[:, TOK_W:2 * TOK_W], gk_ref[...])
        q_ref[rs, :] = q.astype(BF16)
        k_ref[rs, :] = k.astype(BF16)
        v_ref[rs, :] = z[:, 2 * TOK_W:3 * TOK_W].astype(BF16)
        qm_ref[rs, :] = _group_norm64(z[:, 3 * TOK_W:], gqm_ref[...]).astype(BF16)


def _diff_inproj(x2d, g, w_in, gq_tiled, gk_tiled, gqm_tiled):
    n, d = x2d.shape
    tok_spec = pl.BlockSpec((MIX_TM, TOK_W), lambda i: (i, 0))
    tok_shape = jax.ShapeDtypeStruct((n, TOK_W), BF16)
    return pl.pallas_call(
        _diff_inproj_kernel,
        grid=(n // MIX_TM,),
        in_specs=[
            pl.BlockSpec((MIX_TM, d), lambda i: (i, 0)),
            _const_spec((1, d)),
            _const_spec(w_in.shape),
            _const_spec((1, TOK_W)),
            _const_spec((1, TOK_W)),
            _const_spec((1, MEM_W)),
        ],
        out_specs=[tok_spec, tok_spec, tok_spec,
                   pl.BlockSpec((MIX_TM, MEM_W), lambda i: (i, 0))],
        out_shape=[tok_shape, tok_shape, tok_shape,
                   jax.ShapeDtypeStruct((n, MEM_W), BF16)],
        compiler_params=pltpu.CompilerParams(
            dimension_semantics=("arbitrary",), vmem_limit_bytes=VMEM_LIMIT),
        name="diff_inproj",
    )(x2d, g, w_in, gq_tiled, gk_tiled, gqm_tiled)


def _split_components(q):
    lane = lax.broadcasted_iota(jnp.int32, (1, VAL_DIM), 1)
    zero = jnp.zeros_like(q)
    return jnp.concatenate([jnp.where(lane < HEAD_DIM, q, zero),
                            jnp.where(lane >= HEAD_DIM, q, zero)], axis=0)


def _causal_keep(rows, cols):
    r = lax.broadcasted_iota(jnp.int32, (rows, cols), 0)
    c = lax.broadcasted_iota(jnp.int32, (rows, cols), 1)
    return jnp.concatenate([c <= r, c <= r], axis=0)


def _diff_combine(o2, lam_ref, sg_ref, lambda_init):
    rows = o2.shape[0] // 2
    lp = lam_ref[...]
    lam = (jnp.exp(jnp.sum(lp[0:1] * lp[1:2], axis=-1, keepdims=True))
           - jnp.exp(jnp.sum(lp[2:3] * lp[3:4], axis=-1, keepdims=True)) + lambda_init)
    o = o2[:rows] - lam * o2[rows:]
    return (_rms(o, sg_ref[...]) * (1.0 - lambda_init)).astype(BF16)


def _diff_attn_online_kernel(lam_ref, sg_ref, q_ref, k_ref, v_ref, o_ref, *, lambda_init):
    qi = pl.program_id(2)
    qq = _split_components(q_ref[0])

    def scores(j):
        rows = pl.ds(pl.multiple_of(j * ATTN_TK, ATTN_TK), ATTN_TK)
        return _dot_nt(qq, k_ref[0, rows, :]), v_ref[0, rows, :]

    s, vb = scores(qi)
    s = jnp.where(_causal_keep(ATTN_TQ, ATTN_TK), s, -jnp.inf)
    m = jnp.max(s, axis=-1, keepdims=True)
    p = jnp.exp2(s - m)
    l = jnp.sum(p, axis=-1, keepdims=True)
    acc = _dot(p.astype(BF16), vb)

    def body(j, carry):
        m, l, acc = carry
        s, vb = scores(j)
        m_new = jnp.maximum(m, jnp.max(s, axis=-1, keepdims=True))
        alpha = jnp.exp2(m - m_new)
        p = jnp.exp2(s - m_new)
        l = alpha * l + jnp.sum(p, axis=-1, keepdims=True)
        acc = alpha * acc + _dot(p.astype(BF16), vb)
        return m_new, l, acc

    m, l, acc = lax.fori_loop(0, qi, body, (m, l, acc))
    o_ref[0] = _diff_combine(acc / l, lam_ref, sg_ref, lambda_init)


def _diff_attn_bounded_kernel(lam_ref, sg_ref, q_ref, k_ref, v_ref, o_ref, vext_ref,
                              *, lambda_init):
    s_len = q_ref.shape[1]
    lane = lax.broadcasted_iota(jnp.int32, (s_len, VAL_DIM), 1)
    vext_ref[:, :VAL_DIM] = v_ref[0]
    vext_ref[:, VAL_DIM:] = jnp.where(lane == 0, 1.0, 0.0).astype(BF16)
    keep = _causal_keep(ATTN_TQ, ATTN_TQ)

    for i in range(s_len // ATTN_TQ):
        lo = i * ATTN_TQ
        qq = _split_components(q_ref[0, lo:lo + ATTN_TQ, :])
        t = jnp.where(keep, _dot_nt(qq, k_ref[0, lo:lo + ATTN_TQ, :]), -jnp.inf)
        acc = _dot(jnp.exp2(t).astype(BF16), vext_ref[lo:lo + ATTN_TQ, :])
        if i > 0:
            t = _dot_nt(qq, k_ref[0, :lo, :])
            acc = acc + _dot(jnp.exp2(t).astype(BF16), vext_ref[:lo, :])
        o2 = acc[:, :VAL_DIM] * (1.0 / acc[:, VAL_DIM:VAL_DIM + 1])
        o_ref[0, lo:lo + ATTN_TQ, :] = _diff_combine(o2, lam_ref, sg_ref, lambda_init)


def _diff_attn(q, k, v, lam_p, subln_g, lambda_init, logit_bound):
    b, s, _ = q.shape
    out_shape = jax.ShapeDtypeStruct((b, s, TOK_W), BF16)

    def bounded(q, k, v):
        head = pl.BlockSpec((1, s, VAL_DIM), lambda bi, h: (bi, 0, h))
        return pl.pallas_call(
            functools.partial(_diff_attn_bounded_kernel, lambda_init=lambda_init),
            grid=(b, DIFF_HEADS),
            in_specs=[_const_spec(lam_p.shape), _const_spec((1, VAL_DIM)), head, head, head],
            out_specs=head,
            out_shape=out_shape,
            scratch_shapes=[pltpu.VMEM((s, 2 * VAL_DIM), BF16)],
            compiler_params=pltpu.CompilerParams(
                dimension_semantics=("arbitrary", "arbitrary"), vmem_limit_bytes=VMEM_LIMIT),
            name="diff_attn_bounded",
        )(lam_p, subln_g, q, k, v)

    def online(q, k, v):
        head = pl.BlockSpec((1, s, VAL_DIM), lambda bi, h, i: (bi, 0, h))
        blk = pl.BlockSpec((1, ATTN_TQ, VAL_DIM), lambda bi, h, i: (bi, i, h))
        return pl.pallas_call(
            functools.partial(_diff_attn_online_kernel, lambda_init=lambda_init),
            grid=(b, DIFF_HEADS, s // ATTN_TQ),
            in_specs=[_const_spec(lam_p.shape), _const_spec((1, VAL_DIM)), blk, head, head],
            out_specs=blk,
            out_shape=out_shape,
            compiler_params=pltpu.CompilerParams(
                dimension_semantics=("arbitrary", "arbitrary", "arbitrary"),
                vmem_limit_bytes=VMEM_LIMIT),
            name="diff_attn_online",
        )(lam_p, subln_g, q, k, v)

    return lax.cond(logit_bound <= FAST_MAX_LOG2, bounded, online, q, k, v)


def _diff_outproj_kernel(x_ref, tok_ref, qm_ref, k_ref, v_ref, wout_ref, o_ref):
    for r0 in range(0, x_ref.shape[0], MIX_SUB):
        rs = slice(r0, r0 + MIX_SUB)
        mo = _mem_attention(qm_ref[rs, :], k_ref[0], v_ref[0]).astype(BF16)
        y = _dot(tok_ref[rs, :], wout_ref[:TOK_W, :]) + _dot(mo, wout_ref[TOK_W:, :])
        o_ref[rs, :] = x_ref[rs, :] + y


def _diff_outproj(x2d, seq, tok, qm, kmem, vmem, w_out):
    n, d = x2d.shape
    tiles_per_row = seq // MIX_TM
    l = kmem.shape[1]
    return pl.pallas_call(
        _diff_outproj_kernel,
        grid=(n // MIX_TM,),
        in_specs=[
            pl.BlockSpec((MIX_TM, d), lambda i: (i, 0)),
            pl.BlockSpec((MIX_TM, TOK_W), lambda i: (i, 0)),
            pl.BlockSpec((MIX_TM, MEM_W), lambda i: (i, 0)),
            pl.BlockSpec((1, l, MEM_W), lambda i: (i // tiles_per_row, 0, 0)),
            pl.BlockSpec((1, l, MEM_W), lambda i: (i // tiles_per_row, 0, 0)),
            _const_spec(w_out.shape),
        ],
        out_specs=pl.BlockSpec((MIX_TM, d), lambda i: (i, 0)),
        out_shape=jax.ShapeDtypeStruct((n, d), F32),
        compiler_params=pltpu.CompilerParams(
            dimension_semantics=("arbitrary",), vmem_limit_bytes=VMEM_LIMIT),
        name="diff_outproj",
    )(x2d, tok, qm, kmem, vmem, w_out)


def _tile_gain(g, reps):
    return jnp.tile(g.reshape(1, -1), (1, reps))


def kernel(x, mem, ffn_norm, ffn_w_in, ffn_w_out, mix_norm, mem_norm, w_mem_kv,
           memq_norm, memk_norm, w_out, a_w_in, a_v_norm, a_w_s, a_b_s,
           b_w_in, b_q_norm, b_k_norm, b_lambda, b_subln):
    batch, seq, d = x.shape
    depth = ffn_norm.shape[0]
    x2d = x.reshape(batch * seq, d)

    gk_mem = jnp.tile(memk_norm.reshape(depth, 1, HEAD_DIM), (1, 1, MEM_HEADS))
    kmem, vmem = _memkv(mem, mem_norm.reshape(depth, 1, d), w_mem_kv.astype(BF16), gk_mem)

    for i in range(depth):
        j = i // N_MIXERS
        x2d = _ffn(x2d, ffn_norm[i, 0].reshape(1, d),
                   ffn_w_in[i, 0].astype(BF16), ffn_w_out[i, 0].astype(BF16))
        g_mix = mix_norm[i].reshape(1, d)
        gq_mem = _tile_gain(memq_norm[i], MEM_HEADS)
        w_o = w_out[i].astype(BF16)
        if i % N_MIXERS == 0:
            x2d = _gmlp_mix(x2d, seq, g_mix, a_w_in[j].astype(BF16),
                            a_v_norm[j].reshape(1, TOK_W), a_w_s[j], a_b_s[j].T,
                            gq_mem, kmem[i], vmem[i], w_o)
        else:
            lambda_init = 0.8 - 0.6 * math.exp(-0.3 * i)
            q, k, v, qm = _diff_inproj(
                x2d, g_mix, b_w_in[j].astype(BF16),
                _tile_gain(b_q_norm[j], 2 * DIFF_HEADS),
                _tile_gain(b_k_norm[j], 2 * DIFF_HEADS), gq_mem)
            logit_bound = (1.01 * HEAD_DIM * ATTN_SCALE * LOG2E
                           * jnp.max(jnp.abs(b_q_norm[j])) * jnp.max(jnp.abs(b_k_norm[j])))
            tok = _diff_attn(q.reshape(batch, seq, TOK_W), k.reshape(batch, seq, TOK_W),
                             v.reshape(batch, seq, TOK_W), b_lambda[j],
                             b_subln[j].reshape(1, VAL_DIM), lambda_init, logit_bound)
            x2d = _diff_outproj(x2d, seq, tok.reshape(batch * seq, TOK_W), qm,
                                kmem[i], vmem[i], w_o)
        x2d = _ffn(x2d, ffn_norm[i, 1].reshape(1, d),
                   ffn_w_in[i, 1].astype(BF16), ffn_w_out[i, 1].astype(BF16))
    return x2d.reshape(batch, seq, d)
```

```python
import functools
import math

import jax
import jax.numpy as jnp
from jax import lax
from jax.experimental import pallas as pl
from jax.experimental.pallas import tpu as pltpu

D_MODEL = 1024
HEAD_DIM = 64
MEM_HEADS = 4
MEM_W = MEM_HEADS * HEAD_DIM
TOK_W = D_MODEL - MEM_W
CHUNK = 128
GROUP_DIM = 128
GROUPS = TOK_W // GROUP_DIM
DIFF_HEADS = TOK_W // (2 * HEAD_DIM)
VAL_DIM = 2 * HEAD_DIM
N_MIXERS = 2
EPS = 1e-6
ATTN_SCALE = HEAD_DIM ** -0.5
LOG2E = math.log2(math.e)
FAST_MAX_LOG2 = 60.0

LANES = 128
NORM_SLAB = 256

FFN_TM = 512
FFN_CK = 2816
MIX_TM = 1024
MIX_SUB = 256
ATTN_TQ = 256
ATTN_TK = 256
ATTN_HEADS_PER_STEP = 3
VMEM_LIMIT = 56 * 1024 * 1024

F32 = jnp.float32
BF16 = jnp.bfloat16


def _const_spec(shape):
    nd = len(shape)
    return pl.BlockSpec(shape, lambda *_: (0,) * nd, pipeline_mode=pl.Buffered(1))


def _rms(x, g):
    ms = jnp.mean(x * x, axis=-1, keepdims=True)
    return x * lax.rsqrt(ms + EPS) * g


def _dot(a, b):
    return jnp.dot(a, b, preferred_element_type=F32)


def _dot_nt(a, b):
    return lax.dot_general(a, b, (((1,), (1,)), ((), ())), preferred_element_type=F32)


def _block_diag_ones(width, group):
    r = lax.broadcasted_iota(jnp.int32, (width, width), 0) // group
    c = lax.broadcasted_iota(jnp.int32, (width, width), 1) // group
    return jnp.where(r == c, 1.0, 0.0).astype(BF16)


def _group_norm64(x, g):
    bd = _block_diag_ones(NORM_SLAB, HEAD_DIM)
    outs = []
    for j in range(x.shape[-1] // NORM_SLAB):
        xs = x[:, j * NORM_SLAB:(j + 1) * NORM_SLAB]
        sq = xs * xs
        hi = sq.astype(BF16)
        lo = (sq - hi.astype(F32)).astype(BF16)
        ssq = _dot(hi, bd) + _dot(lo, bd)
        outs.append(xs * lax.rsqrt(ssq * (1.0 / HEAD_DIM) + EPS))
    y = outs[0] if len(outs) == 1 else jnp.concatenate(outs, axis=-1)
    return y * g


def _skewed_trace(stages, tiles):
    states = [{} for _ in tiles]
    for t in range(len(tiles) + len(stages) - 1):
        for j in range(len(tiles)):
            if 0 <= t - j < len(stages):
                stages[t - j](states[j], tiles[j])


def _gelu_exact(x):
    return 0.5 * x * (1.0 + lax.erf(x * math.sqrt(0.5)))


def _ffn_kernel(x_ref, g_ref, win_ref, wout_ref, o_ref):
    d_ff = wout_ref.shape[0]
    x = x_ref[...]
    h = _rms(x, g_ref[...]).astype(BF16)
    acc = None
    for lo in range(0, d_ff, FFN_CK):
        hi = min(lo + FFN_CK, d_ff)
        gate = _dot(h, win_ref[:, lo:hi])
        up = _dot(h, win_ref[:, d_ff + lo:d_ff + hi])
        a = (gate * jax.nn.sigmoid(gate) * up).astype(BF16)
        part = _dot(a, wout_ref[lo:hi, :])
        acc = part if acc is None else acc + part
    o_ref[...] = x + 0.5 * acc


def _ffn(x2d, g, w_in, w_out):
    n, d = x2d.shape
    d_ff = w_out.shape[0]
    return pl.pallas_call(
        _ffn_kernel,
        grid=(n // FFN_TM,),
        in_specs=[
            pl.BlockSpec((FFN_TM, d), lambda i: (i, 0)),
            _const_spec((1, d)),
            _const_spec((d, 2 * d_ff)),
            _const_spec((d_ff, d)),
        ],
        out_specs=pl.BlockSpec((FFN_TM, d), lambda i: (i, 0)),
        out_shape=jax.ShapeDtypeStruct((n, d), F32),
        compiler_params=pltpu.CompilerParams(
            dimension_semantics=("arbitrary",), vmem_limit_bytes=VMEM_LIMIT),
        name="ffn",
    )(x2d, g, w_in, w_out)


def _memkv_kernel(mem_ref, g_ref, w_ref, gk_ref, k_ref, v_ref):
    mh = _rms(mem_ref[0], g_ref[0]).astype(BF16)
    kv = _dot(mh, w_ref[0])
    k = _group_norm64(kv[:, :MEM_W], gk_ref[0]) * ATTN_SCALE
    k_ref[0, 0] = k.astype(BF16)
    v_ref[0, 0] = kv[:, MEM_W:].astype(BF16)


def _memkv(mem, mem_norm, w_kv, gk_tiled):
    depth = w_kv.shape[0]
    b, l, d = mem.shape
    out = jax.ShapeDtypeStruct((depth, b, l, MEM_W), BF16)
    return pl.pallas_call(
        _memkv_kernel,
        grid=(depth, b),
        in_specs=[
            pl.BlockSpec((1, l, d), lambda i, j: (j, 0, 0)),
            pl.BlockSpec((1, 1, d), lambda i, j: (i, 0, 0)),
            pl.BlockSpec((1, d, 2 * MEM_W), lambda i, j: (i, 0, 0)),
            pl.BlockSpec((1, 1, MEM_W), lambda i, j: (i, 0, 0)),
        ],
        out_specs=[
            pl.BlockSpec((1, 1, l, MEM_W), lambda i, j: (i, j, 0, 0)),
            pl.BlockSpec((1, 1, l, MEM_W), lambda i, j: (i, j, 0, 0)),
        ],
        out_shape=[out, out],
        compiler_params=pltpu.CompilerParams(
            dimension_semantics=("arbitrary", "arbitrary"), vmem_limit_bytes=VMEM_LIMIT),
        name="memkv",
    )(mem, mem_norm, w_kv, gk_tiled)


def _mem_attention(qn, k, v):
    lane = lax.broadcasted_iota(jnp.int32, (1, MEM_W), 1)
    out = jnp.zeros((qn.shape[0], MEM_W), F32)
    for h in range(MEM_HEADS):
        sel = (lane >= h * HEAD_DIM) & (lane < (h + 1) * HEAD_DIM)
        qh = jnp.where(sel, qn, jnp.zeros_like(qn))
        s = _dot_nt(qh, k)
        p = jnp.exp(s - jnp.max(s, axis=-1, keepdims=True))
        l = jnp.sum(p, axis=-1, keepdims=True)
        vh = jnp.where(sel, v, jnp.zeros_like(v))
        out = out + _dot(p.astype(BF16), vh) / l
    return out


def _gmlp_mix_kernel(x_ref, g_ref, win_ref, vg_ref, ws_ref, bst_ref, gq_ref,
                     k_ref, v_ref, wout_ref, o_ref):
    row = lax.broadcasted_iota(jnp.int32, (CHUNK, CHUNK), 0)
    col = lax.broadcasted_iota(jnp.int32, (CHUNK, CHUNK), 1)
    causal = col <= row
    ws = [jnp.where(causal, ws_ref[g], 0.0).astype(BF16) for g in range(GROUPS)]
    bst = bst_ref[...]

    n_chunks = MIX_SUB // CHUNK

    def in_proj(st, r0):
        st["x"] = x_ref[r0:r0 + MIX_SUB, :]
        h = _rms(st["x"], g_ref[...]).astype(BF16)
        st["z"] = _dot(h, win_ref[...])

    def activate(st, r0):
        z = st.pop("z")
        st["u"] = _gelu_exact(z[:, :TOK_W])
        vact = _gelu_exact(z[:, TOK_W:2 * TOK_W])
        st["vn"] = [_rms(vact[:, g * GROUP_DIM:(g + 1) * GROUP_DIM],
                         vg_ref[:, g * GROUP_DIM:(g + 1) * GROUP_DIM]).astype(BF16)
                    for g in range(GROUPS)]
        st["qn"] = _group_norm64(z[:, 2 * TOK_W:], gq_ref[...]).astype(BF16)

    def spatial(st, r0):
        u, vn = st.pop("u"), st.pop("vn")
        tok_cols = []
        for g in range(GROUPS):
            vg = jnp.concatenate([vn[g][n * CHUNK:(n + 1) * CHUNK] for n in range(n_chunks)],
                                 axis=1)
            mixed = _dot(ws[g], vg) + bst[:, g:g + 1]
            mixed = jnp.concatenate(
                [mixed[:, n * GROUP_DIM:(n + 1) * GROUP_DIM] for n in range(n_chunks)], axis=0)
            tok_cols.append(u[:, g * GROUP_DIM:(g + 1) * GROUP_DIM] * mixed)
        st["tok"] = jnp.concatenate(tok_cols, axis=-1).astype(BF16)

    def mem_attn(st, r0):
        st["mo"] = _mem_attention(st.pop("qn"), k_ref[0], v_ref[0]).astype(BF16)

    def out_proj(st, r0):
        y = _dot(st.pop("tok"), wout_ref[:TOK_W, :]) + _dot(st.pop("mo"), wout_ref[TOK_W:, :])
        o_ref[r0:r0 + MIX_SUB, :] = st.pop("x") + y

    _skewed_trace([in_proj, activate, spatial, mem_attn, out_proj],
                  list(range(0, x_ref.shape[0], MIX_SUB)))


def _gmlp_mix(x2d, seq, g, w_in, v_gain, w_s, b_st, gq_tiled, kmem, vmem, w_out):
    n, d = x2d.shape
    tiles_per_row = seq // MIX_TM
    l = kmem.shape[1]
    return pl.pallas_call(
        _gmlp_mix_kernel,
        grid=(n // MIX_TM,),
        in_specs=[
            pl.BlockSpec((MIX_TM, d), lambda i: (i, 0)),
            _const_spec((1, d)),
            _const_spec(w_in.shape),
            _const_spec((1, TOK_W)),
            _const_spec(w_s.shape),
            _const_spec(b_st.shape),
            _const_spec((1, MEM_W)),
            pl.BlockSpec((1, l, MEM_W), lambda i: (i // tiles_per_row, 0, 0)),
            pl.BlockSpec((1, l, MEM_W), lambda i: (i // tiles_per_row, 0, 0)),
            _const_spec(w_out.shape),
        ],
        out_specs=pl.BlockSpec((MIX_TM, d), lambda i: (i, 0)),
        out_shape=jax.ShapeDtypeStruct((n, d), F32),
        compiler_params=pltpu.CompilerParams(
            dimension_semantics=("arbitrary",), vmem_limit_bytes=VMEM_LIMIT),
        name="gmlp_mix",
    )(x2d, g, w_in, v_gain, w_s, b_st, gq_tiled, kmem, vmem, w_out)


def _diff_inproj_kernel(x_ref, g_ref, win_ref, gq_ref, gk_ref, gqm_ref,
                        q_ref, k_ref, v_ref, qm_ref):
    def in_proj(st, r0):
        h = _rms(x_ref[r0:r0 + MIX_SUB, :], g_ref[...]).astype(BF16)
        st["z"] = _dot(h, win_ref[...])

    def normalise(st, r0):
        rs = slice(r0, r0 + MIX_SUB)
        z = st.pop("z")
        q = _group_norm64(z[:, :TOK_W], gq_ref[...]) * (ATTN_SCALE * LOG2E)
        k = _group_norm64(z[:, TOK_W:2 * TOK_W], gk_ref[...])
        q_ref[rs, :] = q.astype(BF16)
        k_ref[rs, :] = k.astype(BF16)
        v_ref[rs, :] = z[:, 2 * TOK_W:3 * TOK_W].astype(BF16)
        qm_ref[rs, :] = _group_norm64(z[:, 3 * TOK_W:], gqm_ref[...]).astype(BF16)

    _skewed_trace([in_proj, normalise], list(range(0, x_ref.shape[0], MIX_SUB)))


def _diff_inproj(x2d, g, w_in, gq_tiled, gk_tiled, gqm_tiled):
    n, d = x2d.shape
    tok_spec = pl.BlockSpec((MIX_TM, TOK_W), lambda i: (i, 0))
    tok_shape = jax.ShapeDtypeStruct((n, TOK_W), BF16)
    return pl.pallas_call(
        _diff_inproj_kernel,
        grid=(n // MIX_TM,),
        in_specs=[
            pl.BlockSpec((MIX_TM, d), lambda i: (i, 0)),
            _const_spec((1, d)),
            _const_spec(w_in.shape),
            _const_spec((1, TOK_W)),
            _const_spec((1, TOK_W)),
            _const_spec((1, MEM_W)),
        ],
        out_specs=[tok_spec, tok_spec, tok_spec,
                   pl.BlockSpec((MIX_TM, MEM_W), lambda i: (i, 0))],
        out_shape=[tok_shape, tok_shape, tok_shape,
                   jax.ShapeDtypeStruct((n, MEM_W), BF16)],
        compiler_params=pltpu.CompilerParams(
            dimension_semantics=("arbitrary",), vmem_limit_bytes=VMEM_LIMIT),
        name="diff_inproj",
    )(x2d, g, w_in, gq_tiled, gk_tiled, gqm_tiled)


def _split_components(q):
    lane = lax.broadcasted_iota(jnp.int32, (1, VAL_DIM), 1)
    zero = jnp.zeros_like(q)
    return jnp.concatenate([jnp.where(lane < HEAD_DIM, q, zero),
                            jnp.where(lane >= HEAD_DIM, q, zero)], axis=0)


def _causal_keep(rows, cols):
    r = lax.broadcasted_iota(jnp.int32, (rows, cols), 0)
    c = lax.broadcasted_iota(jnp.int32, (rows, cols), 1)
    return jnp.concatenate([c <= r, c <= r], axis=0)


def _diff_combine(o2, lam_ref, sg_ref, lambda_init):
    rows = o2.shape[0] // 2
    lp = lam_ref[...]
    lam = (jnp.exp(jnp.sum(lp[0:1] * lp[1:2], axis=-1, keepdims=True))
           - jnp.exp(jnp.sum(lp[2:3] * lp[3:4], axis=-1, keepdims=True)) + lambda_init)
    o = o2[:rows] - lam * o2[rows:]
    return (_rms(o, sg_ref[...]) * (1.0 - lambda_init)).astype(BF16)


def _diff_attn_online_kernel(lam_ref, sg_ref, q_ref, k_ref, v_ref, o_ref, *, lambda_init):
    qi = pl.program_id(2)
    qq = _split_components(q_ref[0])

    def scores(j):
        rows = pl.ds(pl.multiple_of(j * ATTN_TK, ATTN_TK), ATTN_TK)
        return _dot_nt(qq, k_ref[0, rows, :]), v_ref[0, rows, :]

    s, vb = scores(qi)
    s = jnp.where(_causal_keep(ATTN_TQ, ATTN_TK), s, -jnp.inf)
    m = jnp.max(s, axis=-1, keepdims=True)
    p = jnp.exp2(s - m)
    l = jnp.sum(p, axis=-1, keepdims=True)
    acc = _dot(p.astype(BF16), vb)

    def body(j, carry):
        m, l, acc = carry
        s, vb = scores(j)
        m_new = jnp.maximum(m, jnp.max(s, axis=-1, keepdims=True))
        alpha = jnp.exp2(m - m_new)
        p = jnp.exp2(s - m_new)
        l = alpha * l + jnp.sum(p, axis=-1, keepdims=True)
        acc = alpha * acc + _dot(p.astype(BF16), vb)
        return m_new, l, acc

    m, l, acc = lax.fori_loop(0, qi, body, (m, l, acc))
    o_ref[0] = _diff_combine(acc / l, lam_ref, sg_ref, lambda_init)


def _diff_attn_bounded_kernel(lam_ref, sg_ref, q_ref, k_ref, v_ref, o_ref, vext_ref,
                              *, lambda_init):
    s_len = q_ref.shape[1]
    lane = lax.broadcasted_iota(jnp.int32, (s_len, VAL_DIM), 1)
    ones_col = jnp.where(lane == 0, 1.0, 0.0).astype(BF16)
    keep = _causal_keep(ATTN_TQ, ATTN_TQ)

    for hh in range(q_ref.shape[2] // VAL_DIM):
        hl = slice(hh * VAL_DIM, (hh + 1) * VAL_DIM)
        el = slice(2 * hh * VAL_DIM, 2 * (hh + 1) * VAL_DIM)
        vext_ref[:, 2 * hh * VAL_DIM:(2 * hh + 1) * VAL_DIM] = v_ref[0, :, hl]
        vext_ref[:, (2 * hh + 1) * VAL_DIM:2 * (hh + 1) * VAL_DIM] = ones_col
        for i in range(s_len // ATTN_TQ):
            lo = i * ATTN_TQ
            qq = _split_components(q_ref[0, lo:lo + ATTN_TQ, hl])
            t = jnp.where(keep, _dot_nt(qq, k_ref[0, lo:lo + ATTN_TQ, hl]), -jnp.inf)
            acc = _dot(jnp.exp2(t).astype(BF16), vext_ref[lo:lo + ATTN_TQ, el])
            if i > 0:
                t = _dot_nt(qq, k_ref[0, :lo, hl])
                acc = acc + _dot(jnp.exp2(t).astype(BF16), vext_ref[:lo, el])
            o2 = acc[:, :VAL_DIM] * (1.0 / acc[:, VAL_DIM:VAL_DIM + 1])
            o_ref[0, lo:lo + ATTN_TQ, hl] = _diff_combine(o2, lam_ref, sg_ref, lambda_init)


def _diff_attn(q, k, v, lam_p, subln_g, lambda_init, logit_bound):
    b, s, _ = q.shape
    out_shape = jax.ShapeDtypeStruct((b, s, TOK_W), BF16)

    def bounded(q, k, v):
        width = ATTN_HEADS_PER_STEP * VAL_DIM
        head = pl.BlockSpec((1, s, width), lambda bi, h: (bi, 0, h))
        return pl.pallas_call(
            functools.partial(_diff_attn_bounded_kernel, lambda_init=lambda_init),
            grid=(b, DIFF_HEADS // ATTN_HEADS_PER_STEP),
            in_specs=[_const_spec(lam_p.shape), _const_spec((1, VAL_DIM)), head, head, head],
            out_specs=head,
            out_shape=out_shape,
            scratch_shapes=[pltpu.VMEM((s, 2 * width), BF16)],
            compiler_params=pltpu.CompilerParams(
                dimension_semantics=("arbitrary", "arbitrary"), vmem_limit_bytes=VMEM_LIMIT),
            name="diff_attn_bounded",
        )(lam_p, subln_g, q, k, v)

    def online(q, k, v):
        head = pl.BlockSpec((1, s, VAL_DIM), lambda bi, h, i: (bi, 0, h))
        blk = pl.BlockSpec((1, ATTN_TQ, VAL_DIM), lambda bi, h, i: (bi, i, h))
        return pl.pallas_call(
            functools.partial(_diff_attn_online_kernel, lambda_init=lambda_init),
            grid=(b, DIFF_HEADS, s // ATTN_TQ),
            in_specs=[_const_spec(lam_p.shape), _const_spec((1, VAL_DIM)), blk, head, head],
            out_specs=blk,
            out_shape=out_shape,
            compiler_params=pltpu.CompilerParams(
                dimension_semantics=("arbitrary", "arbitrary", "arbitrary"),
                vmem_limit_bytes=VMEM_LIMIT),
            name="diff_attn_online",
        )(lam_p, subln_g, q, k, v)

    return lax.cond(logit_bound <= FAST_MAX_LOG2, bounded, online, q, k, v)


def _diff_outproj_kernel(x_ref, tok_ref, qm_ref, k_ref, v_ref, wout_ref, o_ref):
    def mem_attn(st, r0):
        st["mo"] = _mem_attention(qm_ref[r0:r0 + MIX_SUB, :], k_ref[0], v_ref[0]).astype(BF16)

    def out_proj(st, r0):
        rs = slice(r0, r0 + MIX_SUB)
        y = _dot(tok_ref[rs, :], wout_ref[:TOK_W, :]) + _dot(st.pop("mo"), wout_ref[TOK_W:, :])
        o_ref[rs, :] = x_ref[rs, :] + y

    _skewed_trace([mem_attn, out_proj], list(range(0, x_ref.shape[0], MIX_SUB)))


def _diff_outproj(x2d, seq, tok, qm, kmem, vmem, w_out):
    n, d = x2d.shape
    tiles_per_row = seq // MIX_TM
    l = kmem.shape[1]
    return pl.pallas_call(
        _diff_outproj_kernel,
        grid=(n // MIX_TM,),
        in_specs=[
            pl.BlockSpec((MIX_TM, d), lambda i: (i, 0)),
            pl.BlockSpec((MIX_TM, TOK_W), lambda i: (i, 0)),
            pl.BlockSpec((MIX_TM, MEM_W), lambda i: (i, 0)),
            pl.BlockSpec((1, l, MEM_W), lambda i: (i // tiles_per_row, 0, 0)),
            pl.BlockSpec((1, l, MEM_W), lambda i: (i // tiles_per_row, 0, 0)),
            _const_spec(w_out.shape),
        ],
        out_specs=pl.BlockSpec((MIX_TM, d), lambda i: (i, 0)),
        out_shape=jax.ShapeDtypeStruct((n, d), F32),
        compiler_params=pltpu.CompilerParams(
            dimension_semantics=("arbitrary",), vmem_limit_bytes=VMEM_LIMIT),
        name="diff_outproj",
    )(x2d, tok, qm, kmem, vmem, w_out)


def _tile_gain(g, reps):
    return jnp.tile(g.reshape(1, -1), (1, reps))


def kernel(x, mem, ffn_norm, ffn_w_in, ffn_w_out, mix_norm, mem_norm, w_mem_kv,
           memq_norm, memk_norm, w_out, a_w_in, a_v_norm, a_w_s, a_b_s,
           b_w_in, b_q_norm, b_k_norm, b_lambda, b_subln):
    batch, seq, d = x.shape
    depth = ffn_norm.shape[0]
    x2d = x.reshape(batch * seq, d)

    gk_mem = jnp.tile(memk_norm.reshape(depth, 1, HEAD_DIM), (1, 1, MEM_HEADS))
    kmem, vmem = _memkv(mem, mem_norm.reshape(depth, 1, d), w_mem_kv.astype(BF16), gk_mem)

    for i in range(depth):
        j = i // N_MIXERS
        x2d = _ffn(x2d, ffn_norm[i, 0].reshape(1, d),
                   ffn_w_in[i, 0].astype(BF16), ffn_w_out[i, 0].astype(BF16))
        g_mix = mix_norm[i].reshape(1, d)
        gq_mem = _tile_gain(memq_norm[i], MEM_HEADS)
        w_o = w_out[i].astype(BF16)
        if i % N_MIXERS == 0:
            x2d = _gmlp_mix(x2d, seq, g_mix, a_w_in[j].astype(BF16),
                            a_v_norm[j].reshape(1, TOK_W), a_w_s[j], a_b_s[j].T,
                            gq_mem, kmem[i], vmem[i], w_o)
        else:
            lambda_init = 0.8 - 0.6 * math.exp(-0.3 * i)
            q, k, v, qm = _diff_inproj(
                x2d, g_mix, b_w_in[j].astype(BF16),
                _tile_gain(b_q_norm[j], 2 * DIFF_HEADS),
                _tile_gain(b_k_norm[j], 2 * DIFF_HEADS), gq_mem)
            logit_bound = (1.01 * HEAD_DIM * ATTN_SCALE * LOG2E
                           * jnp.max(jnp.abs(b_q_norm[j])) * jnp.max(jnp.abs(b_k_norm[j])))
            tok = _diff_attn(q.reshape(batch, seq, TOK_W), k.reshape(batch, seq, TOK_W),
                             v.reshape(batch, seq, TOK_W), b_lambda[j],
                             b_subln[j].reshape(1, VAL_DIM), lambda_init, logit_bound)
            x2d = _diff_outproj(x2d, seq, tok.reshape(batch * seq, TOK_W), qm,
                                kmem[i], vmem[i], w_o)
        x2d = _ffn(x2d, ffn_norm[i, 1].reshape(1, d),
                   ffn_w_in[i, 1].astype(BF16), ffn_w_out[i, 1].astype(BF16))
    return x2d.reshape(batch, seq, d)
```

```python
import functools
import math

import jax
import jax.numpy as jnp
from jax import lax
from jax.experimental import pallas as pl
from jax.experimental.pallas import tpu as pltpu

D_MODEL = 1024
HEAD_DIM = 64
MEM_HEADS = 4
MEM_W = MEM_HEADS * HEAD_DIM
TOK_W = D_MODEL - MEM_W
CHUNK = 128
GROUP_DIM = 128
GROUPS = TOK_W // GROUP_DIM
DIFF_HEADS = TOK_W // (2 * HEAD_DIM)
VAL_DIM = 2 * HEAD_DIM
N_MIXERS = 2
EPS = 1e-6
ATTN_SCALE = HEAD_DIM ** -0.5
LOG2E = math.log2(math.e)
FAST_MAX_LOG2 = 60.0

LANES = 128
NORM_SLAB = 256

FFN_TM = 1024
FFN_SUB = 256
MIX_TM = 1024
MIX_SUB = 256
ATTN_TQ = 256
ATTN_TK = 256
ATTN_HEADS_PER_STEP = 3
VMEM_LIMIT = 56 * 1024 * 1024

F32 = jnp.float32
BF16 = jnp.bfloat16


def _const_spec(shape):
    nd = len(shape)
    return pl.BlockSpec(shape, lambda *_: (0,) * nd, pipeline_mode=pl.Buffered(1))


def _rms(x, g):
    ms = jnp.mean(x * x, axis=-1, keepdims=True)
    return x * lax.rsqrt(ms + EPS) * g


def _dot(a, b):
    return jnp.dot(a, b, preferred_element_type=F32)


def _dot_nt(a, b):
    return lax.dot_general(a, b, (((1,), (1,)), ((), ())), preferred_element_type=F32)


def _block_diag_ones(width, group):
    r = lax.broadcasted_iota(jnp.int32, (width, width), 0) // group
    c = lax.broadcasted_iota(jnp.int32, (width, width), 1) // group
    return jnp.where(r == c, 1.0, 0.0).astype(BF16)


def _group_norm64(x, g):
    bd = _block_diag_ones(NORM_SLAB, HEAD_DIM)
    outs = []
    for j in range(x.shape[-1] // NORM_SLAB):
        xs = x[:, j * NORM_SLAB:(j + 1) * NORM_SLAB]
        ssq = _dot((xs * xs).astype(BF16), bd)
        outs.append(xs * lax.rsqrt(ssq * (1.0 / HEAD_DIM) + EPS))
    y = outs[0] if len(outs) == 1 else jnp.concatenate(outs, axis=-1)
    return y * g


def _skewed_trace(stages, tiles):
    states = [{} for _ in tiles]
    for t in range(len(tiles) + len(stages) - 1):
        for j in range(len(tiles)):
            if 0 <= t - j < len(stages):
                stages[t - j](states[j], tiles[j])


def _gelu_exact(x):
    return 0.5 * x * (1.0 + lax.erf(x * math.sqrt(0.5)))


def _ffn_kernel(x_ref, g_ref, win_ref, wout_ref, o_ref):
    d_ff = wout_ref.shape[0]

    def up_proj(st, r0):
        h = _rms(x_ref[r0:r0 + FFN_SUB, :], g_ref[...]).astype(BF16)
        st["gate"] = _dot(h, win_ref[:, :d_ff])
        st["up"] = _dot(h, win_ref[:, d_ff:])

    def activate(st, r0):
        gate = st.pop("gate")
        st["a"] = (gate * jax.nn.sigmoid(gate) * st.pop("up")).astype(BF16)

    def down_proj(st, r0):
        rs = slice(r0, r0 + FFN_SUB)
        o_ref[rs, :] = x_ref[rs, :] + 0.5 * _dot(st.pop("a"), wout_ref[...])

    _skewed_trace([up_proj, activate, down_proj], list(range(0, x_ref.shape[0], FFN_SUB)))


def _ffn(x2d, g, w_in, w_out):
    n, d = x2d.shape
    d_ff = w_out.shape[0]
    return pl.pallas_call(
        _ffn_kernel,
        grid=(n // FFN_TM,),
        in_specs=[
            pl.BlockSpec((FFN_TM, d), lambda i: (i, 0)),
            _const_spec((1, d)),
            _const_spec((d, 2 * d_ff)),
            _const_spec((d_ff, d)),
        ],
        out_specs=pl.BlockSpec((FFN_TM, d), lambda i: (i, 0)),
        out_shape=jax.ShapeDtypeStruct((n, d), F32),
        compiler_params=pltpu.CompilerParams(
            dimension_semantics=("arbitrary",), vmem_limit_bytes=VMEM_LIMIT),
        name="ffn",
    )(x2d, g, w_in, w_out)


def _memkv_kernel(mem_ref, g_ref, w_ref, gk_ref, k_ref, v_ref):
    mh = _rms(mem_ref[0], g_ref[0]).astype(BF16)
    kv = _dot(mh, w_ref[0])
    k = _group_norm64(kv[:, :MEM_W], gk_ref[0]) * ATTN_SCALE
    k_ref[0, 0] = k.astype(BF16)
    v_ref[0, 0] = kv[:, MEM_W:].astype(BF16)


def _memkv(mem, mem_norm, w_kv, gk_tiled):
    depth = w_kv.shape[0]
    b, l, d = mem.shape
    out = jax.ShapeDtypeStruct((depth, b, l, MEM_W), BF16)
    return pl.pallas_call(
        _memkv_kernel,
        grid=(depth, b),
        in_specs=[
            pl.BlockSpec((1, l, d), lambda i, j: (j, 0, 0)),
            pl.BlockSpec((1, 1, d), lambda i, j: (i, 0, 0)),
            pl.BlockSpec((1, d, 2 * MEM_W), lambda i, j: (i, 0, 0)),
            pl.BlockSpec((1, 1, MEM_W), lambda i, j: (i, 0, 0)),
        ],
        out_specs=[
            pl.BlockSpec((1, 1, l, MEM_W), lambda i, j: (i, j, 0, 0)),
            pl.BlockSpec((1, 1, l, MEM_W), lambda i, j: (i, j, 0, 0)),
        ],
        out_shape=[out, out],
        compiler_params=pltpu.CompilerParams(
            dimension_semantics=("arbitrary", "arbitrary"), vmem_limit_bytes=VMEM_LIMIT),
        name="memkv",
    )(mem, mem_norm, w_kv, gk_tiled)


def _mem_attention(qn, k, v):
    lane = lax.broadcasted_iota(jnp.int32, (1, MEM_W), 1)
    out = jnp.zeros((qn.shape[0], MEM_W), F32)
    for h in range(MEM_HEADS):
        sel = (lane >= h * HEAD_DIM) & (lane < (h + 1) * HEAD_DIM)
        qh = jnp.where(sel, qn, jnp.zeros_like(qn))
        s = _dot_nt(qh, k)
        p = jnp.exp(s - jnp.max(s, axis=-1, keepdims=True))
        l = jnp.sum(p, axis=-1, keepdims=True)
        vh = jnp.where(sel, v, jnp.zeros_like(v))
        out = out + _dot(p.astype(BF16), vh) / l
    return out


def _gmlp_mix_kernel(x_ref, g_ref, win_ref, vg_ref, ws_ref, bst_ref, gq_ref,
                     k_ref, v_ref, wout_ref, o_ref):
    row = lax.broadcasted_iota(jnp.int32, (CHUNK, CHUNK), 0)
    col = lax.broadcasted_iota(jnp.int32, (CHUNK, CHUNK), 1)
    causal = col <= row
    ws = [jnp.where(causal, ws_ref[g], 0.0).astype(BF16) for g in range(GROUPS)]
    bst = bst_ref[...]

    n_chunks = MIX_SUB // CHUNK

    def in_proj(st, r0):
        st["x"] = x_ref[r0:r0 + MIX_SUB, :]
        h = _rms(st["x"], g_ref[...]).astype(BF16)
        st["z"] = _dot(h, win_ref[...])

    def activate(st, r0):
        z = st.pop("z")
        st["u"] = _gelu_exact(z[:, :TOK_W])
        vact = _gelu_exact(z[:, TOK_W:2 * TOK_W])
        st["vn"] = [_rms(vact[:, g * GROUP_DIM:(g + 1) * GROUP_DIM],
                         vg_ref[:, g * GROUP_DIM:(g + 1) * GROUP_DIM]).astype(BF16)
                    for g in range(GROUPS)]
        st["qn"] = _group_norm64(z[:, 2 * TOK_W:], gq_ref[...]).astype(BF16)

    def spatial(st, r0):
        u, vn = st.pop("u"), st.pop("vn")
        tok_cols = []
        for g in range(GROUPS):
            vg = jnp.concatenate([vn[g][n * CHUNK:(n + 1) * CHUNK] for n in range(n_chunks)],
                                 axis=1)
            mixed = _dot(ws[g], vg) + bst[:, g:g + 1]
            mixed = jnp.concatenate(
                [mixed[:, n * GROUP_DIM:(n + 1) * GROUP_DIM] for n in range(n_chunks)], axis=0)
            tok_cols.append(u[:, g * GROUP_DIM:(g + 1) * GROUP_DIM] * mixed)
        st["tok"] = jnp.concatenate(tok_cols, axis=-1).astype(BF16)

    def mem_attn(st, r0):
        st["mo"] = _mem_attention(st.pop("qn"), k_ref[0], v_ref[0]).astype(BF16)

    def out_proj(st, r0):
        y = _dot(st.pop("tok"), wout_ref[:TOK_W, :]) + _dot(st.pop("mo"), wout_ref[TOK_W:, :])
        o_ref[r0:r0 + MIX_SUB, :] = st.pop("x") + y

    _skewed_trace([in_proj, activate, spatial, mem_attn, out_proj],
                  list(range(0, x_ref.shape[0], MIX_SUB)))


def _gmlp_mix(x2d, seq, g, w_in, v_gain, w_s, b_st, gq_tiled, kmem, vmem, w_out):
    n, d = x2d.shape
    tiles_per_row = seq // MIX_TM
    l = kmem.shape[1]
    return pl.pallas_call(
        _gmlp_mix_kernel,
        grid=(n // MIX_TM,),
        in_specs=[
            pl.BlockSpec((MIX_TM, d), lambda i: (i, 0)),
            _const_spec((1, d)),
            _const_spec(w_in.shape),
            _const_spec((1, TOK_W)),
            _const_spec(w_s.shape),
            _const_spec(b_st.shape),
            _const_spec((1, MEM_W)),
            pl.BlockSpec((1, l, MEM_W), lambda i: (i // tiles_per_row, 0, 0)),
            pl.BlockSpec((1, l, MEM_W), lambda i: (i // tiles_per_row, 0, 0)),
            _const_spec(w_out.shape),
        ],
        out_specs=pl.BlockSpec((MIX_TM, d), lambda i: (i, 0)),
        out_shape=jax.ShapeDtypeStruct((n, d), F32),
        compiler_params=pltpu.CompilerParams(
            dimension_semantics=("arbitrary",), vmem_limit_bytes=VMEM_LIMIT),
        name="gmlp_mix",
    )(x2d, g, w_in, v_gain, w_s, b_st, gq_tiled, kmem, vmem, w_out)


def _diff_inproj_kernel(x_ref, g_ref, win_ref, gq_ref, gk_ref, gqm_ref,
                        q_ref, k_ref, v_ref, qm_ref):
    def in_proj(st, r0):
        h = _rms(x_ref[r0:r0 + MIX_SUB, :], g_ref[...]).astype(BF16)
        st["z"] = _dot(h, win_ref[...])

    def normalise(st, r0):
        rs = slice(r0, r0 + MIX_SUB)
        z = st.pop("z")
        q = _group_norm64(z[:, :TOK_W], gq_ref[...]) * (ATTN_SCALE * LOG2E)
        k = _group_norm64(z[:, TOK_W:2 * TOK_W], gk_ref[...])
        q_ref[rs, :] = q.astype(BF16)
        k_ref[rs, :] = k.astype(BF16)
        v_ref[rs, :] = z[:, 2 * TOK_W:3 * TOK_W].astype(BF16)
        qm_ref[rs, :] = _group_norm64(z[:, 3 * TOK_W:], gqm_ref[...]).astype(BF16)

    _skewed_trace([in_proj, normalise], list(range(0, x_ref.shape[0], MIX_SUB)))


def _diff_inproj(x2d, g, w_in, gq_tiled, gk_tiled, gqm_tiled):
    n, d = x2d.shape
    tok_spec = pl.BlockSpec((MIX_TM, TOK_W), lambda i: (i, 0))
    tok_shape = jax.ShapeDtypeStruct((n, TOK_W), BF16)
    return pl.pallas_call(
        _diff_inproj_kernel,
        grid=(n // MIX_TM,),
        in_specs=[
            pl.BlockSpec((MIX_TM, d), lambda i: (i, 0)),
            _const_spec((1, d)),
            _const_spec(w_in.shape),
            _const_spec((1, TOK_W)),
            _const_spec((1, TOK_W)),
            _const_spec((1, MEM_W)),
        ],
        out_specs=[tok_spec, tok_spec, tok_spec,
                   pl.BlockSpec((MIX_TM, MEM_W), lambda i: (i, 0))],
        out_shape=[tok_shape, tok_shape, tok_shape,
                   jax.ShapeDtypeStruct((n, MEM_W), BF16)],
        compiler_params=pltpu.CompilerParams(
            dimension_semantics=("arbitrary",), vmem_limit_bytes=VMEM_LIMIT),
        name="diff_inproj",
    )(x2d, g, w_in, gq_tiled, gk_tiled, gqm_tiled)


def _split_components(q):
    lane = lax.broadcasted_iota(jnp.int32, (1, VAL_DIM), 1)
    zero = jnp.zeros_like(q)
    return jnp.concatenate([jnp.where(lane < HEAD_DIM, q, zero),
                            jnp.where(lane >= HEAD_DIM, q, zero)], axis=0)


def _causal_keep(rows, cols):
    r = lax.broadcasted_iota(jnp.int32, (rows, cols), 0)
    c = lax.broadcasted_iota(jnp.int32, (rows, cols), 1)
    return jnp.concatenate([c <= r, c <= r], axis=0)


def _diff_combine(o2, lam_ref, sg_ref, lambda_init):
    rows = o2.shape[0] // 2
    lp = lam_ref[...]
    lam = (jnp.exp(jnp.sum(lp[0:1] * lp[1:2], axis=-1, keepdims=True))
           - jnp.exp(jnp.sum(lp[2:3] * lp[3:4], axis=-1, keepdims=True)) + lambda_init)
    o = o2[:rows] - lam * o2[rows:]
    return (_rms(o, sg_ref[...]) * (1.0 - lambda_init)).astype(BF16)


def _diff_attn_online_kernel(lam_ref, sg_ref, q_ref, k_ref, v_ref, o_ref, *, lambda_init):
    qi = pl.program_id(2)
    qq = _split_components(q_ref[0])

    def scores(j):
        rows = pl.ds(pl.multiple_of(j * ATTN_TK, ATTN_TK), ATTN_TK)
        return _dot_nt(qq, k_ref[0, rows, :]), v_ref[0, rows, :]

    s, vb = scores(qi)
    s = jnp.where(_causal_keep(ATTN_TQ, ATTN_TK), s, -jnp.inf)
    m = jnp.max(s, axis=-1, keepdims=True)
    p = jnp.exp2(s - m)
    l = jnp.sum(p, axis=-1, keepdims=True)
    acc = _dot(p.astype(BF16), vb)

    def body(j, carry):
        m, l, acc = carry
        s, vb = scores(j)
        m_new = jnp.maximum(m, jnp.max(s, axis=-1, keepdims=True))
        alpha = jnp.exp2(m - m_new)
        p = jnp.exp2(s - m_new)
        l = alpha * l + jnp.sum(p, axis=-1, keepdims=True)
        acc = alpha * acc + _dot(p.astype(BF16), vb)
        return m_new, l, acc

    m, l, acc = lax.fori_loop(0, qi, body, (m, l, acc))
    o_ref[0] = _diff_combine(acc / l, lam_ref, sg_ref, lambda_init)


def _diff_attn_bounded_kernel(lam_ref, sg_ref, q_ref, k_ref, v_ref, o_ref, vext_ref,
                              *, lambda_init):
    s_len = q_ref.shape[1]
    lane = lax.broadcasted_iota(jnp.int32, (s_len, VAL_DIM), 1)
    ones_col = jnp.where(lane == 0, 1.0, 0.0).astype(BF16)
    keep = _causal_keep(ATTN_TQ, ATTN_TQ)

    for hh in range(q_ref.shape[2] // VAL_DIM):
        hl = slice(hh * VAL_DIM, (hh + 1) * VAL_DIM)
        el = slice(2 * hh * VAL_DIM, 2 * (hh + 1) * VAL_DIM)
        vext_ref[:, 2 * hh * VAL_DIM:(2 * hh + 1) * VAL_DIM] = v_ref[0, :, hl]
        vext_ref[:, (2 * hh + 1) * VAL_DIM:2 * (hh + 1) * VAL_DIM] = ones_col
        for i in range(s_len // ATTN_TQ):
            lo = i * ATTN_TQ
            qq = _split_components(q_ref[0, lo:lo + ATTN_TQ, hl])
            t = jnp.where(keep, _dot_nt(qq, k_ref[0, lo:lo + ATTN_TQ, hl]), -jnp.inf)
            acc = _dot(jnp.exp2(t).astype(BF16), vext_ref[lo:lo + ATTN_TQ, el])
            if i > 0:
                t = _dot_nt(qq, k_ref[0, :lo, hl])
                acc = acc + _dot(jnp.exp2(t).astype(BF16), vext_ref[:lo, el])
            o2 = acc[:, :VAL_DIM] * (1.0 / acc[:, VAL_DIM:VAL_DIM + 1])
            o_ref[0, lo:lo + ATTN_TQ, hl] = _diff_combine(o2, lam_ref, sg_ref, lambda_init)


def _diff_attn(q, k, v, lam_p, subln_g, lambda_init, logit_bound):
    b, s, _ = q.shape
    out_shape = jax.ShapeDtypeStruct((b, s, TOK_W), BF16)

    def bounded(q, k, v):
        width = ATTN_HEADS_PER_STEP * VAL_DIM
        head = pl.BlockSpec((1, s, width), lambda bi, h: (bi, 0, h))
        return pl.pallas_call(
            functools.partial(_diff_attn_bounded_kernel, lambda_init=lambda_init),
            grid=(b, DIFF_HEADS // ATTN_HEADS_PER_STEP),
            in_specs=[_const_spec(lam_p.shape), _const_spec((1, VAL_DIM)), head, head, head],
            out_specs=head,
            out_shape=out_shape,
            scratch_shapes=[pltpu.VMEM((s, 2 * width), BF16)],
            compiler_params=pltpu.CompilerParams(
                dimension_semantics=("arbitrary", "arbitrary"), vmem_limit_bytes=VMEM_LIMIT),
            name="diff_attn_bounded",
        )(lam_p, subln_g, q, k, v)

    def online(q, k, v):
        head = pl.BlockSpec((1, s, VAL_DIM), lambda bi, h, i: (bi, 0, h))
        blk = pl.BlockSpec((1, ATTN_TQ, VAL_DIM), lambda bi, h, i: (bi, i, h))
        return pl.pallas_call(
            functools.partial(_diff_attn_online_kernel, lambda_init=lambda_init),
            grid=(b, DIFF_HEADS, s // ATTN_TQ),
            in_specs=[_const_spec(lam_p.shape), _const_spec((1, VAL_DIM)), blk, head, head],
            out_specs=blk,
            out_shape=out_shape,
            compiler_params=pltpu.CompilerParams(
                dimension_semantics=("arbitrary", "arbitrary", "arbitrary"),
                vmem_limit_bytes=VMEM_LIMIT),
            name="diff_attn_online",
        )(lam_p, subln_g, q, k, v)

    return lax.cond(logit_bound <= FAST_MAX_LOG2, bounded, online, q, k, v)


def _diff_outproj_kernel(x_ref, tok_ref, qm_ref, k_ref, v_ref, wout_ref, o_ref):
    def mem_attn(st, r0):
        st["mo"] = _mem_attention(qm_ref[r0:r0 + MIX_SUB, :], k_ref[0], v_ref[0]).astype(BF16)

    def out_proj(st, r0):
        rs = slice(r0, r0 + MIX_SUB)
        y = _dot(tok_ref[rs, :], wout_ref[:TOK_W, :]) + _dot(st.pop("mo"), wout_ref[TOK_W:, :])
        o_ref[rs, :] = x_ref[rs, :] + y

    _skewed_trace([mem_attn, out_proj], list(range(0, x_ref.shape[0], MIX_SUB)))


def _diff_outproj(x2d, seq, tok, qm, kmem, vmem, w_out):
    n, d = x2d.shape
    tiles_per_row = seq // MIX_TM
    l = kmem.shape[1]
    return pl.pallas_call(
        _diff_outproj_kernel,
        grid=(n // MIX_TM,),
        in_specs=[
            pl.BlockSpec((MIX_TM, d), lambda i: (i, 0)),
            pl.BlockSpec((MIX_TM, TOK_W), lambda i: (i, 0)),
            pl.BlockSpec((MIX_TM, MEM_W), lambda i: (i, 0)),
            pl.BlockSpec((1, l, MEM_W), lambda i: (i // tiles_per_row, 0, 0)),
            pl.BlockSpec((1, l, MEM_W), lambda i: (i // tiles_per_row, 0, 0)),
            _const_spec(w_out.shape),
        ],
        out_specs=pl.BlockSpec((MIX_TM, d), lambda i: (i, 0)),
        out_shape=jax.ShapeDtypeStruct((n, d), F32),
        compiler_params=pltpu.CompilerParams(
            dimension_semantics=("arbitrary",), vmem_limit_bytes=VMEM_LIMIT),
        name="diff_outproj",
    )(x2d, tok, qm, kmem, vmem, w_out)


def _tile_gain(g, reps):
    return jnp.tile(g.reshape(1, -1), (1, reps))


def kernel(x, mem, ffn_norm, ffn_w_in, ffn_w_out, mix_norm, mem_norm, w_mem_kv,
           memq_norm, memk_norm, w_out, a_w_in, a_v_norm, a_w_s, a_b_s,
           b_w_in, b_q_norm, b_k_norm, b_lambda, b_subln):
    batch, seq, d = x.shape
    depth = ffn_norm.shape[0]
    x2d = x.reshape(batch * seq, d)

    gk_mem = jnp.tile(memk_norm.reshape(depth, 1, HEAD_DIM), (1, 1, MEM_HEADS))
    kmem, vmem = _memkv(mem, mem_norm.reshape(depth, 1, d), w_mem_kv.astype(BF16), gk_mem)

    for i in range(depth):
        j = i // N_MIXERS
        x2d = _ffn(x2d, ffn_norm[i, 0].reshape(1, d),
                   ffn_w_in[i, 0].astype(BF16), ffn_w_out[i, 0].astype(BF16))
        g_mix = mix_norm[i].reshape(1, d)
        gq_mem = _tile_gain(memq_norm[i], MEM_HEADS)
        w_o = w_out[i].astype(BF16)
        if i % N_MIXERS == 0:
            x2d = _gmlp_mix(x2d, seq, g_mix, a_w_in[j].astype(BF16),
                            a_v_norm[j].reshape(1, TOK_W), a_w_s[j], a_b_s[j].T,
                            gq_mem, kmem[i], vmem[i], w_o)
        else:
            lambda_init = 0.8 - 0.6 * math.exp(-0.3 * i)
            q, k, v, qm = _diff_inproj(
                x2d, g_mix, b_w_in[j].astype(BF16),
                _tile_gain(b_q_norm[j], 2 * DIFF_HEADS),
                _tile_gain(b_k_norm[j], 2 * DIFF_HEADS), gq_mem)
            logit_bound = (1.01 * HEAD_DIM * ATTN_SCALE * LOG2E
                           * jnp.max(jnp.abs(b_q_norm[j])) * jnp.max(jnp.abs(b_k_norm[j])))
            tok = _diff_attn(q.reshape(batch, seq, TOK_W), k.reshape(batch, seq, TOK_W),
                             v.reshape(batch, seq, TOK_W), b_lambda[j],
                             b_subln[j].reshape(1, VAL_DIM), lambda_init, logit_bound)
            x2d = _diff_outproj(x2d, seq, tok.reshape(batch * seq, TOK_W), qm,
                                kmem[i], vmem[i], w_o)
        x2d = _ffn(x2d, ffn_norm[i, 1].reshape(1, d),
                   ffn_w_in[i, 1].astype(BF16), ffn_w_out[i, 1].astype(BF16))
    return x2d.reshape(batch, seq, d)
```

```python
import functools
import math

import jax
import jax.numpy as jnp
from jax import lax
from jax.experimental import pallas as pl
from jax.experimental.pallas import tpu as pltpu

D_MODEL = 1024
HEAD_DIM = 64
MEM_HEADS = 4
MEM_W = MEM_HEADS * HEAD_DIM
TOK_W = D_MODEL - MEM_W
CHUNK = 128
GROUP_DIM = 128
GROUPS = TOK_W // GROUP_DIM
DIFF_HEADS = TOK_W // (2 * HEAD_DIM)
VAL_DIM = 2 * HEAD_DIM
N_MIXERS = 2
EPS = 1e-6
ATTN_SCALE = HEAD_DIM ** -0.5
LOG2E = math.log2(math.e)
FAST_MAX_LOG2 = 60.0

LANES = 128
NORM_SLAB = 256

SUB = 256
TM = 1024
BF16_ROWS = 16
ATTN_TQ = 256
ATTN_TK = 256
ATTN_HEADS_PER_STEP = 3
VMEM_LIMIT = 56 * 1024 * 1024

F32 = jnp.float32
BF16 = jnp.bfloat16


def _const_spec(shape):
    nd = len(shape)
    return pl.BlockSpec(shape, lambda *_: (0,) * nd, pipeline_mode=pl.Buffered(1))


def _rms(x, g):
    ms = jnp.mean(x * x, axis=-1, keepdims=True)
    return x * lax.rsqrt(ms + EPS) * g


def _dot(a, b):
    return jnp.dot(a, b, preferred_element_type=F32)


def _dot_nt(a, b):
    return lax.dot_general(a, b, (((1,), (1,)), ((), ())), preferred_element_type=F32)


def _block_diag_ones(width, group):
    r = lax.broadcasted_iota(jnp.int32, (width, width), 0) // group
    c = lax.broadcasted_iota(jnp.int32, (width, width), 1) // group
    return jnp.where(r == c, 1.0, 0.0).astype(BF16)


def _group_norm64(x, g):
    bd = _block_diag_ones(NORM_SLAB, HEAD_DIM)
    outs = []
    for j in range(x.shape[-1] // NORM_SLAB):
        xs = x[:, j * NORM_SLAB:(j + 1) * NORM_SLAB]
        ssq = _dot((xs * xs).astype(BF16), bd)
        outs.append(xs * lax.rsqrt(ssq * (1.0 / HEAD_DIM) + EPS))
    y = outs[0] if len(outs) == 1 else jnp.concatenate(outs, axis=-1)
    return y * g


def _skewed_trace(stages, tiles):
    states = [{} for _ in tiles]
    for t in range(len(tiles) + len(stages) - 1):
        for j in range(len(tiles)):
            if 0 <= t - j < len(stages):
                stages[t - j](states[j], tiles[j])


def _gelu_exact(x):
    return 0.5 * x * (1.0 + lax.erf(x * math.sqrt(0.5)))


def _read(ref):
    return lambda st, r0: ref[r0:r0 + SUB, :]


def _write(ref):
    def write(st, r0, val):
        ref[r0:r0 + SUB, :] = val
    return write


def _ffn_stages(read_x, write_out, g_ref, win_ref, wout_ref):
    d_ff = wout_ref.shape[0]

    def up_proj(st, r0):
        st["ffn_x"] = read_x(st, r0)
        h = _rms(st["ffn_x"], g_ref[...]).astype(BF16)
        st["gate"] = _dot(h, win_ref[:, :d_ff])
        st["up"] = _dot(h, win_ref[:, d_ff:])

    def activate(st, r0):
        gate = st.pop("gate")
        st["act"] = (gate * jax.nn.sigmoid(gate) * st.pop("up")).astype(BF16)

    def down_proj(st, r0):
        write_out(st, r0, st.pop("ffn_x") + 0.5 * _dot(st.pop("act"), wout_ref[...]))

    return [up_proj, activate, down_proj]


def _mem_attention(qn, k, v):
    lane = lax.broadcasted_iota(jnp.int32, (1, MEM_W), 1)
    out = jnp.zeros((qn.shape[0], MEM_W), F32)
    for h in range(MEM_HEADS):
        sel = (lane >= h * HEAD_DIM) & (lane < (h + 1) * HEAD_DIM)
        qh = jnp.where(sel, qn, jnp.zeros_like(qn))
        s = _dot_nt(qh, k)
        p = jnp.exp(s - jnp.max(s, axis=-1, keepdims=True))
        l = jnp.sum(p, axis=-1, keepdims=True)
        vh = jnp.where(sel, v, jnp.zeros_like(v))
        out = out + _dot(p.astype(BF16), vh) / l
    return out


def _gmlp_stages(read_x, write_out, g_ref, win_ref, vg_ref, ws_ref, bst_ref, gq_ref,
                 k_ref, v_ref, wout_ref):
    n_chunks = SUB // CHUNK
    row = lax.broadcasted_iota(jnp.int32, (CHUNK, CHUNK), 0)
    col = lax.broadcasted_iota(jnp.int32, (CHUNK, CHUNK), 1)
    ws = [jnp.where(col <= row, ws_ref[g], 0.0).astype(BF16) for g in range(GROUPS)]
    bst = bst_ref[...]

    def in_proj(st, r0):
        st["mix_x"] = read_x(st, r0)
        h = _rms(st["mix_x"], g_ref[...]).astype(BF16)
        st["z"] = _dot(h, win_ref[...])

    def activate(st, r0):
        z = st.pop("z")
        st["u"] = _gelu_exact(z[:, :TOK_W])
        vact = _gelu_exact(z[:, TOK_W:2 * TOK_W])
        st["vn"] = [_rms(vact[:, g * GROUP_DIM:(g + 1) * GROUP_DIM],
                         vg_ref[:, g * GROUP_DIM:(g + 1) * GROUP_DIM]).astype(BF16)
                    for g in range(GROUPS)]
        st["qn"] = _group_norm64(z[:, 2 * TOK_W:], gq_ref[...]).astype(BF16)

    def spatial(st, r0):
        u, vn = st.pop("u"), st.pop("vn")
        tok_cols = []
        for g in range(GROUPS):
            vg = jnp.concatenate([vn[g][n * CHUNK:(n + 1) * CHUNK] for n in range(n_chunks)],
                                 axis=1)
            mixed = _dot(ws[g], vg) + bst[:, g:g + 1]
            mixed = jnp.concatenate(
                [mixed[:, n * GROUP_DIM:(n + 1) * GROUP_DIM] for n in range(n_chunks)], axis=0)
            tok_cols.append(u[:, g * GROUP_DIM:(g + 1) * GROUP_DIM] * mixed)
        st["tok"] = jnp.concatenate(tok_cols, axis=-1).astype(BF16)

    def mem_attn(st, r0):
        st["mo"] = _mem_attention(st.pop("qn"), k_ref[0], v_ref[0]).astype(BF16)

    def out_proj(st, r0):
        y = _dot(st.pop("tok"), wout_ref[:TOK_W, :]) + _dot(st.pop("mo"), wout_ref[TOK_W:, :])
        write_out(st, r0, st.pop("mix_x") + y)

    return [in_proj, activate, spatial, mem_attn, out_proj]


def _diff_inproj_stages(read_x, g_ref, win_ref, gq_ref, gk_ref, gqm_ref,
                        q_ref, k_ref, v_ref, qm_ref):
    def in_proj(st, r0):
        h = _rms(read_x(st, r0), g_ref[...]).astype(BF16)
        st["z"] = _dot(h, win_ref[...])

    def normalise(st, r0):
        rs = slice(r0, r0 + SUB)
        z = st.pop("z")
        q = _group_norm64(z[:, :TOK_W], gq_ref[...]) * (ATTN_SCALE * LOG2E)
        k = _group_norm64(z[:, TOK_W:2 * TOK_W], gk_ref[...])
        q_ref[rs, :] = q.astype(BF16)
        k_ref[rs, :] = k.astype(BF16)
        v_ref[rs, :] = z[:, 2 * TOK_W:3 * TOK_W].astype(BF16)
        qm_ref[rs, :] = _group_norm64(z[:, 3 * TOK_W:], gqm_ref[...]).astype(BF16)

    return [in_proj, normalise]


def _diff_outproj_stages(read_x, write_out, tok_ref, qm_ref, k_ref, v_ref, wout_ref):
    def mem_attn(st, r0):
        st["mo"] = _mem_attention(qm_ref[r0:r0 + SUB, :], k_ref[0], v_ref[0]).astype(BF16)

    def out_proj(st, r0):
        y = (_dot(tok_ref[r0:r0 + SUB, :], wout_ref[:TOK_W, :])
             + _dot(st.pop("mo"), wout_ref[TOK_W:, :]))
        write_out(st, r0, read_x(st, r0) + y)

    return [mem_attn, out_proj]


def _run_stages(stages, rows):
    _skewed_trace(stages, list(range(0, rows, SUB)))


def _ffn_kernel(x_ref, fg_ref, fwin_ref, fwout_ref, o_ref):
    _run_stages(_ffn_stages(_read(x_ref), _write(o_ref), fg_ref, fwin_ref, fwout_ref),
                x_ref.shape[0])


def _gmlp_kernel(x_ref, g_ref, win_ref, vg_ref, ws_ref, bst_ref, gq_ref, k_ref, v_ref,
                 wout_ref, o_ref):
    _run_stages(_gmlp_stages(_read(x_ref), _write(o_ref), g_ref, win_ref, vg_ref, ws_ref,
                             bst_ref, gq_ref, k_ref, v_ref, wout_ref), x_ref.shape[0])


def _diff_inproj_kernel(x_ref, g_ref, win_ref, gq_ref, gk_ref, gqm_ref,
                        q_ref, k_ref, v_ref, qm_ref):
    _run_stages(_diff_inproj_stages(_read(x_ref), g_ref, win_ref, gq_ref, gk_ref, gqm_ref,
                                    q_ref, k_ref, v_ref, qm_ref), x_ref.shape[0])


def _diff_outproj_kernel(x_ref, tok_ref, qm_ref, k_ref, v_ref, wout_ref, o_ref):
    _run_stages(_diff_outproj_stages(_read(x_ref), _write(o_ref), tok_ref, qm_ref, k_ref,
                                     v_ref, wout_ref), x_ref.shape[0])


def _cast_blocks(rows, n_steps):
    for blocks in range(n_steps, 0, -1):
        if n_steps % blocks == 0 and rows % (blocks * BF16_ROWS) == 0:
            return blocks
    raise ValueError(f"{rows} rows cannot be split into bf16 row blocks over {n_steps} steps")


def _cast_rider(stacked, lead, n_steps, flat_step):
    rows, cols = stacked.shape[len(lead):]
    blocks = _cast_blocks(rows, n_steps)
    steps_per_block = n_steps // blocks
    squeezed = (None,) * len(lead)

    def src_index(*g):
        return lead + (flat_step(*g) // steps_per_block, 0)

    def dst_index(*g):
        return (flat_step(*g) // steps_per_block, 0)

    src = pl.BlockSpec(squeezed + (rows // blocks, cols), src_index)
    dst = pl.BlockSpec((rows // blocks, cols), dst_index)
    return (stacked, src), (jax.ShapeDtypeStruct((rows, cols), BF16), dst)


def _with_riders(kernel_fn, n_in, n_out, n_riders):
    def kern(*refs):
        ins, rin = refs[:n_in], refs[n_in:n_in + n_riders]
        outs = refs[n_in + n_riders:n_in + n_riders + n_out]
        rout = refs[n_in + n_riders + n_out:n_in + 2 * n_riders + n_out]
        for src, dst in zip(rin, rout):
            dst[...] = src[...].astype(BF16)
        kernel_fn(*ins, *outs, *refs[n_in + 2 * n_riders + n_out:])
    return kern


def _ffn_weight_riders(ffn_w_in, ffn_w_out, lead, n_steps, flat_step):
    return [_cast_rider(ffn_w_in, lead, n_steps, flat_step),
            _cast_rider(ffn_w_out, lead, n_steps, flat_step)]


def _row_spec(width):
    return pl.BlockSpec((TM, width), lambda i: (i, 0))


def _layer_const(stacked, lead):
    tail = stacked.shape[len(lead):]
    spec = pl.BlockSpec((None,) * len(lead) + tail, lambda *_: lead + (0,) * len(tail),
                        pipeline_mode=pl.Buffered(1))
    return (stacked, spec)


def _mem_spec(stacked, layer, seq):
    tiles_per_row = seq // TM
    _, _, l, w = stacked.shape
    return (stacked, pl.BlockSpec((None, 1, l, w), lambda i: (layer, i // tiles_per_row, 0, 0)))


def _const(arr):
    return (arr, _const_spec(arr.shape))


def _token_call(kernel_fn, name, n, ins, outs, riders=()):
    rin = [r[0] for r in riders]
    rout = [r[1] for r in riders]
    if riders:
        kernel_fn = _with_riders(kernel_fn, len(ins), len(outs), len(riders))
    return pl.pallas_call(
        kernel_fn,
        grid=(n // TM,),
        in_specs=[spec for _, spec in ins + rin],
        out_specs=[spec for _, spec in outs + rout],
        out_shape=[shape for shape, _ in outs + rout],
        compiler_params=pltpu.CompilerParams(
            dimension_semantics=("arbitrary",), vmem_limit_bytes=VMEM_LIMIT),
        name=name,
    )(*[arr for arr, _ in ins + rin])


def _ffn(x2d, gain, w_in, w_out, riders):
    n, d = x2d.shape
    return _token_call(
        _ffn_kernel, "ffn", n,
        [(x2d, _row_spec(d)), gain, _const(w_in), _const(w_out)],
        [(jax.ShapeDtypeStruct((n, d), F32), _row_spec(d))], riders)


def _gmlp_mix(x2d, mix, kmem, vmem, w_o):
    n, d = x2d.shape
    (out,) = _token_call(
        _gmlp_kernel, "gmlp_mix", n,
        [(x2d, _row_spec(d))] + mix + [kmem, vmem, w_o],
        [(jax.ShapeDtypeStruct((n, d), F32), _row_spec(d))])
    return out


def _diff_inproj(x2d, mix):
    n, d = x2d.shape
    tok = (jax.ShapeDtypeStruct((n, TOK_W), BF16), _row_spec(TOK_W))
    return _token_call(
        _diff_inproj_kernel, "diff_inproj", n, [(x2d, _row_spec(d))] + mix,
        [tok, tok, tok, (jax.ShapeDtypeStruct((n, MEM_W), BF16), _row_spec(MEM_W))])


def _diff_outproj(x2d, tok, qm, kmem, vmem, w_o):
    n, d = x2d.shape
    (out,) = _token_call(
        _diff_outproj_kernel, "diff_outproj", n,
        [(x2d, _row_spec(d)), (tok, _row_spec(TOK_W)), (qm, _row_spec(MEM_W)), kmem, vmem, w_o],
        [(jax.ShapeDtypeStruct((n, d), F32), _row_spec(d))])
    return out


def _memkv_kernel(mem_ref, g_ref, w_ref, gk_ref, k_ref, v_ref):
    mh = _rms(mem_ref[0], g_ref[0]).astype(BF16)
    kv = _dot(mh, w_ref[0])
    k = _group_norm64(kv[:, :MEM_W], gk_ref[0]) * ATTN_SCALE
    k_ref[0, 0] = k.astype(BF16)
    v_ref[0, 0] = kv[:, MEM_W:].astype(BF16)


def _memkv(mem, mem_norm, w_kv, gk_tiled, make_riders):
    depth = w_kv.shape[0]
    b, l, d = mem.shape
    out = jax.ShapeDtypeStruct((depth, b, l, MEM_W), BF16)
    kv_spec = pl.BlockSpec((1, 1, l, MEM_W), lambda i, j: (i, j, 0, 0))
    riders = make_riders(depth * b, lambda i, j: i * b + j)
    ins = [(mem, pl.BlockSpec((1, l, d), lambda i, j: (j, 0, 0))),
           (mem_norm, pl.BlockSpec((1, 1, d), lambda i, j: (i, 0, 0))),
           (w_kv, pl.BlockSpec((1, d, 2 * MEM_W), lambda i, j: (i, 0, 0))),
           (gk_tiled, pl.BlockSpec((1, 1, MEM_W), lambda i, j: (i, 0, 0)))]
    ins += [r[0] for r in riders]
    outs = [(out, kv_spec), (out, kv_spec)] + [r[1] for r in riders]
    return pl.pallas_call(
        _with_riders(_memkv_kernel, 4, 2, len(riders)),
        grid=(depth, b),
        in_specs=[spec for _, spec in ins],
        out_specs=[spec for _, spec in outs],
        out_shape=[shape for shape, _ in outs],
        compiler_params=pltpu.CompilerParams(
            dimension_semantics=("arbitrary", "arbitrary"), vmem_limit_bytes=VMEM_LIMIT),
        name="memkv",
    )(*[arr for arr, _ in ins])


def _split_components(q):
    lane = lax.broadcasted_iota(jnp.int32, (1, VAL_DIM), 1)
    zero = jnp.zeros_like(q)
    return jnp.concatenate([jnp.where(lane < HEAD_DIM, q, zero),
                            jnp.where(lane >= HEAD_DIM, q, zero)], axis=0)


def _causal_keep(rows, cols):
    r = lax.broadcasted_iota(jnp.int32, (rows, cols), 0)
    c = lax.broadcasted_iota(jnp.int32, (rows, cols), 1)
    return jnp.concatenate([c <= r, c <= r], axis=0)


def _diff_combine(o2, lam_ref, sg_ref, lambda_init):
    rows = o2.shape[0] // 2
    lp = lam_ref[...]
    lam = (jnp.exp(jnp.sum(lp[0:1] * lp[1:2], axis=-1, keepdims=True))
           - jnp.exp(jnp.sum(lp[2:3] * lp[3:4], axis=-1, keepdims=True)) + lambda_init)
    o = o2[:rows] - lam * o2[rows:]
    return (_rms(o, sg_ref[...]) * (1.0 - lambda_init)).astype(BF16)


def _diff_attn_online_kernel(lam_ref, sg_ref, q_ref, k_ref, v_ref, o_ref, *, lambda_init):
    qi = pl.program_id(2)
    qq = _split_components(q_ref[0])

    def scores(j):
        rows = pl.ds(pl.multiple_of(j * ATTN_TK, ATTN_TK), ATTN_TK)
        return _dot_nt(qq, k_ref[0, rows, :]), v_ref[0, rows, :]

    s, vb = scores(qi)
    s = jnp.where(_causal_keep(ATTN_TQ, ATTN_TK), s, -jnp.inf)
    m = jnp.max(s, axis=-1, keepdims=True)
    p = jnp.exp2(s - m)
    l = jnp.sum(p, axis=-1, keepdims=True)
    acc = _dot(p.astype(BF16), vb)

    def body(j, carry):
        m, l, acc = carry
        s, vb = scores(j)
        m_new = jnp.maximum(m, jnp.max(s, axis=-1, keepdims=True))
        alpha = jnp.exp2(m - m_new)
        p = jnp.exp2(s - m_new)
        l = alpha * l + jnp.sum(p, axis=-1, keepdims=True)
        acc = alpha * acc + _dot(p.astype(BF16), vb)
        return m_new, l, acc

    m, l, acc = lax.fori_loop(0, qi, body, (m, l, acc))
    o_ref[0] = _diff_combine(acc / l, lam_ref, sg_ref, lambda_init)


def _diff_attn_bounded_kernel(lam_ref, sg_ref, q_ref, k_ref, v_ref, o_ref, vext_ref,
                              *, lambda_init):
    s_len = q_ref.shape[1]
    lane = lax.broadcasted_iota(jnp.int32, (s_len, VAL_DIM), 1)
    ones_col = jnp.where(lane == 0, 1.0, 0.0).astype(BF16)
    keep = _causal_keep(ATTN_TQ, ATTN_TQ)

    for hh in range(q_ref.shape[2] // VAL_DIM):
        hl = slice(hh * VAL_DIM, (hh + 1) * VAL_DIM)
        el = slice(2 * hh * VAL_DIM, 2 * (hh + 1) * VAL_DIM)
        vext_ref[:, 2 * hh * VAL_DIM:(2 * hh + 1) * VAL_DIM] = v_ref[0, :, hl]
        vext_ref[:, (2 * hh + 1) * VAL_DIM:2 * (hh + 1) * VAL_DIM] = ones_col
        for i in range(s_len // ATTN_TQ):
            lo = i * ATTN_TQ
            qq = _split_components(q_ref[0, lo:lo + ATTN_TQ, hl])
            t = jnp.where(keep, _dot_nt(qq, k_ref[0, lo:lo + ATTN_TQ, hl]), -jnp.inf)
            acc = _dot(jnp.exp2(t).astype(BF16), vext_ref[lo:lo + ATTN_TQ, el])
            if i > 0:
                t = _dot_nt(qq, k_ref[0, :lo, hl])
                acc = acc + _dot(jnp.exp2(t).astype(BF16), vext_ref[:lo, el])
            o2 = acc[:, :VAL_DIM] * (1.0 / acc[:, VAL_DIM:VAL_DIM + 1])
            o_ref[0, lo:lo + ATTN_TQ, hl] = _diff_combine(o2, lam_ref, sg_ref, lambda_init)


def _diff_attn(q, k, v, lam_p, subln_g, lambda_init, logit_bound):
    b, s, _ = q.shape
    out_shape = jax.ShapeDtypeStruct((b, s, TOK_W), BF16)

    def bounded(q, k, v):
        width = ATTN_HEADS_PER_STEP * VAL_DIM
        head = pl.BlockSpec((1, s, width), lambda bi, h: (bi, 0, h))
        return pl.pallas_call(
            functools.partial(_diff_attn_bounded_kernel, lambda_init=lambda_init),
            grid=(b, DIFF_HEADS // ATTN_HEADS_PER_STEP),
            in_specs=[_const_spec(lam_p.shape), _const_spec((1, VAL_DIM)), head, head, head],
            out_specs=head,
            out_shape=out_shape,
            scratch_shapes=[pltpu.VMEM((s, 2 * width), BF16)],
            compiler_params=pltpu.CompilerParams(
                dimension_semantics=("arbitrary", "arbitrary"), vmem_limit_bytes=VMEM_LIMIT),
            name="diff_attn_bounded",
        )(lam_p, subln_g, q, k, v)

    def online(q, k, v):
        head = pl.BlockSpec((1, s, VAL_DIM), lambda bi, h, i: (bi, 0, h))
        blk = pl.BlockSpec((1, ATTN_TQ, VAL_DIM), lambda bi, h, i: (bi, i, h))
        return pl.pallas_call(
            functools.partial(_diff_attn_online_kernel, lambda_init=lambda_init),
            grid=(b, DIFF_HEADS, s // ATTN_TQ),
            in_specs=[_const_spec(lam_p.shape), _const_spec((1, VAL_DIM)), blk, head, head],
            out_specs=blk,
            out_shape=out_shape,
            compiler_params=pltpu.CompilerParams(
                dimension_semantics=("arbitrary", "arbitrary", "arbitrary"),
                vmem_limit_bytes=VMEM_LIMIT),
            name="diff_attn_online",
        )(lam_p, subln_g, q, k, v)

    return lax.cond(logit_bound <= FAST_MAX_LOG2, bounded, online, q, k, v)


def _tile_gain(g, reps):
    return jnp.tile(g.reshape(1, -1), (1, reps))


def kernel(x, mem, ffn_norm, ffn_w_in, ffn_w_out, mix_norm, mem_norm, w_mem_kv,
           memq_norm, memk_norm, w_out, a_w_in, a_v_norm, a_w_s, a_b_s,
           b_w_in, b_q_norm, b_k_norm, b_lambda, b_subln):
    batch, seq, d = x.shape
    depth = ffn_norm.shape[0]
    x2d = x.reshape(batch * seq, d)

    n_steps = (batch * seq) // TM
    ffn_order = [(i, half) for i in range(depth) for half in range(2)]
    ffn_gain = ffn_norm.reshape(depth, 2, 1, d)

    gk_mem = jnp.tile(memk_norm.reshape(depth, 1, HEAD_DIM), (1, 1, MEM_HEADS))
    kmem, vmem, w_in_bf, w_out_bf = _memkv(
        mem, mem_norm.reshape(depth, 1, d), w_mem_kv.astype(BF16), gk_mem,
        lambda steps, flat: _ffn_weight_riders(ffn_w_in, ffn_w_out, ffn_order[0], steps, flat))
    w_mix_out = w_out.astype(BF16)

    def ffn(x2d, pos, w_in_bf, w_out_bf):
        riders = []
        if pos + 1 < len(ffn_order):
            riders = _ffn_weight_riders(ffn_w_in, ffn_w_out, ffn_order[pos + 1], n_steps,
                                        lambda i: i)
        res = _ffn(x2d, _layer_const(ffn_gain, ffn_order[pos]), w_in_bf, w_out_bf, riders)
        return res if riders else (res[0], None, None)

    for i in range(depth):
        j = i // N_MIXERS
        g_mix = _const(mix_norm[i].reshape(1, d))
        gq_mem = _const(_tile_gain(memq_norm[i], MEM_HEADS))
        k_i, v_i = _mem_spec(kmem, i, seq), _mem_spec(vmem, i, seq)
        w_o = _layer_const(w_mix_out, (i,))
        x2d, w_in_bf, w_out_bf = ffn(x2d, 2 * i, w_in_bf, w_out_bf)
        if i % N_MIXERS == 0:
            mix = [g_mix, _const(a_w_in[j].astype(BF16)), _const(a_v_norm[j].reshape(1, TOK_W)),
                   _const(a_w_s[j]), _const(a_b_s[j].T), gq_mem]
            x2d = _gmlp_mix(x2d, mix, k_i, v_i, w_o)
        else:
            lambda_init = 0.8 - 0.6 * math.exp(-0.3 * i)
            mix = [g_mix, _const(b_w_in[j].astype(BF16)),
                   _const(_tile_gain(b_q_norm[j], 2 * DIFF_HEADS)),
                   _const(_tile_gain(b_k_norm[j], 2 * DIFF_HEADS)), gq_mem]
            q, k, v, qm = _diff_inproj(x2d, mix)
            logit_bound = (1.01 * HEAD_DIM * ATTN_SCALE * LOG2E
                           * jnp.max(jnp.abs(b_q_norm[j])) * jnp.max(jnp.abs(b_k_norm[j])))
            tok = _diff_attn(q.reshape(batch, seq, TOK_W), k.reshape(batch, seq, TOK_W),
                             v.reshape(batch, seq, TOK_W), b_lambda[j],
                             b_subln[j].reshape(1, VAL_DIM), lambda_init, logit_bound)
            x2d = _diff_outproj(x2d, tok.reshape(batch * seq, TOK_W), qm, k_i, v_i, w_o)
        x2d, w_in_bf, w_out_bf = ffn(x2d, 2 * i + 1, w_in_bf, w_out_bf)
    return x2d.reshape(batch, seq, d)
```

```python
import functools
import math

import jax
import jax.numpy as jnp
from jax import lax
from jax.experimental import pallas as pl
from jax.experimental.pallas import tpu as pltpu

D_MODEL = 1024
HEAD_DIM = 64
MEM_HEADS = 4
MEM_W = MEM_HEADS * HEAD_DIM
TOK_W = D_MODEL - MEM_W
CHUNK = 128
GROUP_DIM = 128
GROUPS = TOK_W // GROUP_DIM
DIFF_HEADS = TOK_W // (2 * HEAD_DIM)
VAL_DIM = 2 * HEAD_DIM
N_MIXERS = 2
EPS = 1e-6
ATTN_SCALE = HEAD_DIM ** -0.5
LOG2E = math.log2(math.e)
FAST_MAX_LOG2 = 60.0

LANES = 128
NORM_SLAB = 256

SUB = 256
TM = 1024
BF16_ROWS = 16
ATTN_TQ = 256
ATTN_TK = 256
ATTN_HEADS_PER_STEP = 3
VT_ROWS = VAL_DIM + BF16_ROWS
VMEM_LIMIT = 56 * 1024 * 1024

F32 = jnp.float32
BF16 = jnp.bfloat16


def _const_spec(shape):
    nd = len(shape)
    return pl.BlockSpec(shape, lambda *_: (0,) * nd, pipeline_mode=pl.Buffered(1))


def _rms(x, g):
    ms = jnp.mean(x * x, axis=-1, keepdims=True)
    return x * lax.rsqrt(ms + EPS) * g


def _dot(a, b):
    return jnp.dot(a, b, preferred_element_type=F32)


def _dot_nt(a, b):
    return lax.dot_general(a, b, (((1,), (1,)), ((), ())), preferred_element_type=F32)


def _block_diag_ones(width, group):
    r = lax.broadcasted_iota(jnp.int32, (width, width), 0) // group
    c = lax.broadcasted_iota(jnp.int32, (width, width), 1) // group
    return jnp.where(r == c, 1.0, 0.0).astype(BF16)


def _group_norm64(x, g):
    bd = _block_diag_ones(NORM_SLAB, HEAD_DIM)
    outs = []
    for j in range(x.shape[-1] // NORM_SLAB):
        xs = x[:, j * NORM_SLAB:(j + 1) * NORM_SLAB]
        ssq = _dot((xs * xs).astype(BF16), bd)
        outs.append(xs * lax.rsqrt(ssq * (1.0 / HEAD_DIM) + EPS))
    y = outs[0] if len(outs) == 1 else jnp.concatenate(outs, axis=-1)
    return y * g


def _skewed_trace(stages, tiles):
    states = [{} for _ in tiles]
    for t in range(len(tiles) + len(stages) - 1):
        for j in range(len(tiles)):
            if 0 <= t - j < len(stages):
                stages[t - j](states[j], tiles[j])


def _gelu_exact(x):
    return 0.5 * x * (1.0 + lax.erf(x * math.sqrt(0.5)))


def _read(ref):
    return lambda st, r0: ref[r0:r0 + SUB, :]


def _write(ref):
    def write(st, r0, val):
        ref[r0:r0 + SUB, :] = val
    return write


def _ffn_stages(read_x, write_out, g_ref, win_ref, wout_ref):
    d_ff = wout_ref.shape[0]

    def up_proj(st, r0):
        st["ffn_x"] = read_x(st, r0)
        h = _rms(st["ffn_x"], g_ref[...]).astype(BF16)
        st["gate"] = _dot(h, win_ref[:, :d_ff])
        st["up"] = _dot(h, win_ref[:, d_ff:])

    def activate(st, r0):
        gate = st.pop("gate")
        st["act"] = (gate * jax.nn.sigmoid(gate) * st.pop("up")).astype(BF16)

    def down_proj(st, r0):
        write_out(st, r0, st.pop("ffn_x") + 0.5 * _dot(st.pop("act"), wout_ref[...]))

    return [up_proj, activate, down_proj]


def _mem_attention(qn, k, v):
    lane = lax.broadcasted_iota(jnp.int32, (1, MEM_W), 1)
    out = jnp.zeros((qn.shape[0], MEM_W), F32)
    for h in range(MEM_HEADS):
        sel = (lane >= h * HEAD_DIM) & (lane < (h + 1) * HEAD_DIM)
        qh = jnp.where(sel, qn, jnp.zeros_like(qn))
        s = _dot_nt(qh, k)
        p = jnp.exp(s - jnp.max(s, axis=-1, keepdims=True))
        l = jnp.sum(p, axis=-1, keepdims=True)
        vh = jnp.where(sel, v, jnp.zeros_like(v))
        out = out + _dot(p.astype(BF16), vh) / l
    return out


def _gmlp_stages(read_x, write_out, g_ref, win_ref, vg_ref, ws_ref, bst_ref, gq_ref,
                 k_ref, v_ref, wout_ref):
    n_chunks = SUB // CHUNK
    row = lax.broadcasted_iota(jnp.int32, (CHUNK, CHUNK), 0)
    col = lax.broadcasted_iota(jnp.int32, (CHUNK, CHUNK), 1)
    ws = [jnp.where(col <= row, ws_ref[g], 0.0).astype(BF16) for g in range(GROUPS)]
    bst = bst_ref[...]

    def in_proj(st, r0):
        st["mix_x"] = read_x(st, r0)
        h = _rms(st["mix_x"], g_ref[...]).astype(BF16)
        st["z"] = _dot(h, win_ref[...])

    def activate(st, r0):
        z = st.pop("z")
        st["u"] = _gelu_exact(z[:, :TOK_W])
        vact = _gelu_exact(z[:, TOK_W:2 * TOK_W])
        st["vn"] = [_rms(vact[:, g * GROUP_DIM:(g + 1) * GROUP_DIM],
                         vg_ref[:, g * GROUP_DIM:(g + 1) * GROUP_DIM]).astype(BF16)
                    for g in range(GROUPS)]
        st["qn"] = _group_norm64(z[:, 2 * TOK_W:], gq_ref[...]).astype(BF16)

    def spatial(st, r0):
        u, vn = st.pop("u"), st.pop("vn")
        tok_cols = []
        for g in range(GROUPS):
            vg = jnp.concatenate([vn[g][n * CHUNK:(n + 1) * CHUNK] for n in range(n_chunks)],
                                 axis=1)
            mixed = _dot(ws[g], vg) + bst[:, g:g + 1]
            mixed = jnp.concatenate(
                [mixed[:, n * GROUP_DIM:(n + 1) * GROUP_DIM] for n in range(n_chunks)], axis=0)
            tok_cols.append(u[:, g * GROUP_DIM:(g + 1) * GROUP_DIM] * mixed)
        st["tok"] = jnp.concatenate(tok_cols, axis=-1).astype(BF16)

    def mem_attn(st, r0):
        st["mo"] = _mem_attention(st.pop("qn"), k_ref[0], v_ref[0]).astype(BF16)

    def out_proj(st, r0):
        y = _dot(st.pop("tok"), wout_ref[:TOK_W, :]) + _dot(st.pop("mo"), wout_ref[TOK_W:, :])
        write_out(st, r0, st.pop("mix_x") + y)

    return [in_proj, activate, spatial, mem_attn, out_proj]


def _diff_inproj_stages(read_x, g_ref, win_ref, gq_ref, gk_ref, gqm_ref,
                        q_ref, k_ref, v_ref, qm_ref):
    def in_proj(st, r0):
        h = _rms(read_x(st, r0), g_ref[...]).astype(BF16)
        st["z"] = _dot(h, win_ref[...])

    def normalise(st, r0):
        rs = slice(r0, r0 + SUB)
        z = st.pop("z")
        q = _group_norm64(z[:, :TOK_W], gq_ref[...]) * (ATTN_SCALE * LOG2E)
        k = _group_norm64(z[:, TOK_W:2 * TOK_W], gk_ref[...])
        q_ref[rs, :] = q.astype(BF16)
        k_ref[rs, :] = k.astype(BF16)
        v_ref[rs, :] = z[:, 2 * TOK_W:3 * TOK_W].astype(BF16)
        qm_ref[rs, :] = _group_norm64(z[:, 3 * TOK_W:], gqm_ref[...]).astype(BF16)

    return [in_proj, normalise]


def _diff_outproj_stages(read_x, write_out, tok_ref, qm_ref, k_ref, v_ref, wout_ref):
    def mem_attn(st, r0):
        st["mo"] = _mem_attention(qm_ref[r0:r0 + SUB, :], k_ref[0], v_ref[0]).astype(BF16)

    def out_proj(st, r0):
        y = (_dot(tok_ref[r0:r0 + SUB, :], wout_ref[:TOK_W, :])
             + _dot(st.pop("mo"), wout_ref[TOK_W:, :]))
        write_out(st, r0, read_x(st, r0) + y)

    return [mem_attn, out_proj]


def _run_stages(stages, rows):
    _skewed_trace(stages, list(range(0, rows, SUB)))


def _ffn_kernel(x_ref, fg_ref, fwin_ref, fwout_ref, o_ref):
    _run_stages(_ffn_stages(_read(x_ref), _write(o_ref), fg_ref, fwin_ref, fwout_ref),
                x_ref.shape[0])


def _gmlp_kernel(x_ref, g_ref, win_ref, vg_ref, ws_ref, bst_ref, gq_ref, k_ref, v_ref,
                 wout_ref, o_ref):
    _run_stages(_gmlp_stages(_read(x_ref), _write(o_ref), g_ref, win_ref, vg_ref, ws_ref,
                             bst_ref, gq_ref, k_ref, v_ref, wout_ref), x_ref.shape[0])


def _diff_inproj_kernel(x_ref, g_ref, win_ref, gq_ref, gk_ref, gqm_ref,
                        q_ref, k_ref, v_ref, qm_ref):
    _run_stages(_diff_inproj_stages(_read(x_ref), g_ref, win_ref, gq_ref, gk_ref, gqm_ref,
                                    q_ref, k_ref, v_ref, qm_ref), x_ref.shape[0])


def _diff_outproj_kernel(x_ref, tok_ref, qm_ref, k_ref, v_ref, wout_ref, o_ref):
    _run_stages(_diff_outproj_stages(_read(x_ref), _write(o_ref), tok_ref, qm_ref, k_ref,
                                     v_ref, wout_ref), x_ref.shape[0])


def _cast_blocks(rows, n_steps):
    for blocks in range(n_steps, 0, -1):
        if n_steps % blocks == 0 and rows % (blocks * BF16_ROWS) == 0:
            return blocks
    raise ValueError(f"{rows} rows cannot be split into bf16 row blocks over {n_steps} steps")


def _cast_rider(stacked, lead, n_steps, flat_step):
    rows, cols = stacked.shape[len(lead):]
    blocks = _cast_blocks(rows, n_steps)
    steps_per_block = n_steps // blocks
    squeezed = (None,) * len(lead)

    def src_index(*g):
        return lead + (flat_step(*g) // steps_per_block, 0)

    def dst_index(*g):
        return (flat_step(*g) // steps_per_block, 0)

    src = pl.BlockSpec(squeezed + (rows // blocks, cols), src_index)
    dst = pl.BlockSpec((rows // blocks, cols), dst_index)
    return (stacked, src), (jax.ShapeDtypeStruct((rows, cols), BF16), dst)


def _with_riders(kernel_fn, n_in, n_out, n_riders):
    def kern(*refs):
        ins, rin = refs[:n_in], refs[n_in:n_in + n_riders]
        outs = refs[n_in + n_riders:n_in + n_riders + n_out]
        rout = refs[n_in + n_riders + n_out:n_in + 2 * n_riders + n_out]
        for src, dst in zip(rin, rout):
            dst[...] = src[...].astype(BF16)
        kernel_fn(*ins, *outs, *refs[n_in + 2 * n_riders + n_out:])
    return kern


def _ffn_weight_riders(ffn_w_in, ffn_w_out, lead, n_steps, flat_step):
    return [_cast_rider(ffn_w_in, lead, n_steps, flat_step),
            _cast_rider(ffn_w_out, lead, n_steps, flat_step)]


def _row_spec(width):
    return pl.BlockSpec((TM, width), lambda i: (i, 0))


def _layer_const(stacked, lead):
    tail = stacked.shape[len(lead):]
    spec = pl.BlockSpec((None,) * len(lead) + tail, lambda *_: lead + (0,) * len(tail),
                        pipeline_mode=pl.Buffered(1))
    return (stacked, spec)


def _mem_spec(stacked, layer, seq):
    tiles_per_row = seq // TM
    _, _, l, w = stacked.shape
    return (stacked, pl.BlockSpec((None, 1, l, w), lambda i: (layer, i // tiles_per_row, 0, 0)))


def _const(arr):
    return (arr, _const_spec(arr.shape))


def _token_call(kernel_fn, name, n, ins, outs, riders=()):
    rin = [r[0] for r in riders]
    rout = [r[1] for r in riders]
    if riders:
        kernel_fn = _with_riders(kernel_fn, len(ins), len(outs), len(riders))
    return pl.pallas_call(
        kernel_fn,
        grid=(n // TM,),
        in_specs=[spec for _, spec in ins + rin],
        out_specs=[spec for _, spec in outs + rout],
        out_shape=[shape for shape, _ in outs + rout],
        compiler_params=pltpu.CompilerParams(
            dimension_semantics=("arbitrary",), vmem_limit_bytes=VMEM_LIMIT),
        name=name,
    )(*[arr for arr, _ in ins + rin])


def _ffn(x2d, gain, w_in, w_out, riders):
    n, d = x2d.shape
    return _token_call(
        _ffn_kernel, "ffn", n,
        [(x2d, _row_spec(d)), gain, _const(w_in), _const(w_out)],
        [(jax.ShapeDtypeStruct((n, d), F32), _row_spec(d))], riders)


def _gmlp_mix(x2d, mix, kmem, vmem, w_o):
    n, d = x2d.shape
    (out,) = _token_call(
        _gmlp_kernel, "gmlp_mix", n,
        [(x2d, _row_spec(d))] + mix + [kmem, vmem, w_o],
        [(jax.ShapeDtypeStruct((n, d), F32), _row_spec(d))])
    return out


def _diff_inproj(x2d, mix):
    n, d = x2d.shape
    tok = (jax.ShapeDtypeStruct((n, TOK_W), BF16), _row_spec(TOK_W))
    return _token_call(
        _diff_inproj_kernel, "diff_inproj", n, [(x2d, _row_spec(d))] + mix,
        [tok, tok, tok, (jax.ShapeDtypeStruct((n, MEM_W), BF16), _row_spec(MEM_W))])


def _diff_outproj(x2d, tok, qm, kmem, vmem, w_o):
    n, d = x2d.shape
    (out,) = _token_call(
        _diff_outproj_kernel, "diff_outproj", n,
        [(x2d, _row_spec(d)), (tok, _row_spec(TOK_W)), (qm, _row_spec(MEM_W)), kmem, vmem, w_o],
        [(jax.ShapeDtypeStruct((n, d), F32), _row_spec(d))])
    return out


def _memkv_kernel(mem_ref, g_ref, w_ref, gk_ref, k_ref, v_ref):
    mh = _rms(mem_ref[0], g_ref[0]).astype(BF16)
    kv = _dot(mh, w_ref[0])
    k = _group_norm64(kv[:, :MEM_W], gk_ref[0]) * ATTN_SCALE
    k_ref[0, 0] = k.astype(BF16)
    v_ref[0, 0] = kv[:, MEM_W:].astype(BF16)


def _memkv(mem, mem_norm, w_kv, gk_tiled, make_riders):
    depth = w_kv.shape[0]
    b, l, d = mem.shape
    out = jax.ShapeDtypeStruct((depth, b, l, MEM_W), BF16)
    kv_spec = pl.BlockSpec((1, 1, l, MEM_W), lambda i, j: (i, j, 0, 0))
    riders = make_riders(depth * b, lambda i, j: i * b + j)
    ins = [(mem, pl.BlockSpec((1, l, d), lambda i, j: (j, 0, 0))),
           (mem_norm, pl.BlockSpec((1, 1, d), lambda i, j: (i, 0, 0))),
           (w_kv, pl.BlockSpec((1, d, 2 * MEM_W), lambda i, j: (i, 0, 0))),
           (gk_tiled, pl.BlockSpec((1, 1, MEM_W), lambda i, j: (i, 0, 0)))]
    ins += [r[0] for r in riders]
    outs = [(out, kv_spec), (out, kv_spec)] + [r[1] for r in riders]
    return pl.pallas_call(
        _with_riders(_memkv_kernel, 4, 2, len(riders)),
        grid=(depth, b),
        in_specs=[spec for _, spec in ins],
        out_specs=[spec for _, spec in outs],
        out_shape=[shape for shape, _ in outs],
        compiler_params=pltpu.CompilerParams(
            dimension_semantics=("arbitrary", "arbitrary"), vmem_limit_bytes=VMEM_LIMIT),
        name="memkv",
    )(*[arr for arr, _ in ins])


def _split_components(q):
    lane = lax.broadcasted_iota(jnp.int32, (1, VAL_DIM), 1)
    zero = jnp.zeros_like(q)
    return jnp.concatenate([jnp.where(lane < HEAD_DIM, q, zero),
                            jnp.where(lane >= HEAD_DIM, q, zero)], axis=0)


def _causal_keep(rows, cols):
    r = lax.broadcasted_iota(jnp.int32, (rows, cols), 0)
    c = lax.broadcasted_iota(jnp.int32, (rows, cols), 1)
    return jnp.concatenate([c <= r, c <= r], axis=0)


def _diff_combine(o2, lam_ref, sg_ref, lambda_init):
    rows = o2.shape[0] // 2
    lp = lam_ref[...]
    lam = (jnp.exp(jnp.sum(lp[0:1] * lp[1:2], axis=-1, keepdims=True))
           - jnp.exp(jnp.sum(lp[2:3] * lp[3:4], axis=-1, keepdims=True)) + lambda_init)
    o = o2[:rows] - lam * o2[rows:]
    return (_rms(o, sg_ref[...]) * (1.0 - lambda_init)).astype(BF16)


def _diff_attn_online_kernel(lam_ref, sg_ref, q_ref, k_ref, v_ref, o_ref, *, lambda_init):
    qi = pl.program_id(2)
    qq = _split_components(q_ref[0])

    def scores(j):
        rows = pl.ds(pl.multiple_of(j * ATTN_TK, ATTN_TK), ATTN_TK)
        return _dot_nt(qq, k_ref[0, rows, :]), v_ref[0, rows, :]

    s, vb = scores(qi)
    s = jnp.where(_causal_keep(ATTN_TQ, ATTN_TK), s, -jnp.inf)
    m = jnp.max(s, axis=-1, keepdims=True)
    p = jnp.exp2(s - m)
    l = jnp.sum(p, axis=-1, keepdims=True)
    acc = _dot(p.astype(BF16), vb)

    def body(j, carry):
        m, l, acc = carry
        s, vb = scores(j)
        m_new = jnp.maximum(m, jnp.max(s, axis=-1, keepdims=True))
        alpha = jnp.exp2(m - m_new)
        p = jnp.exp2(s - m_new)
        l = alpha * l + jnp.sum(p, axis=-1, keepdims=True)
        acc = alpha * acc + _dot(p.astype(BF16), vb)
        return m_new, l, acc

    m, l, acc = lax.fori_loop(0, qi, body, (m, l, acc))
    o_ref[0] = _diff_combine(acc / l, lam_ref, sg_ref, lambda_init)


def _diff_attn_bounded_kernel(lam_ref, sg_ref, q_ref, k_ref, v_ref, o_ref, vext_ref,
                              *, lambda_init):
    s_len = q_ref.shape[1]
    row = lax.broadcasted_iota(jnp.int32, (VT_ROWS - VAL_DIM, s_len), 0)
    ones_row = jnp.where(row == 0, 1.0, 0.0).astype(BF16)
    key_i = lax.broadcasted_iota(jnp.int32, (ATTN_TQ, ATTN_TQ), 0)
    qry_i = lax.broadcasted_iota(jnp.int32, (ATTN_TQ, ATTN_TQ), 1)
    keep = jnp.concatenate([key_i <= qry_i, key_i <= qry_i], axis=1)
    lp = lam_ref[...]
    lam = (jnp.exp(jnp.sum(lp[0:1] * lp[1:2], axis=-1, keepdims=True))
           - jnp.exp(jnp.sum(lp[2:3] * lp[3:4], axis=-1, keepdims=True)) + lambda_init)
    gain_t = sg_ref[...] * (1.0 - lambda_init)

    for hh in range(q_ref.shape[2] // VAL_DIM):
        hl = slice(hh * VAL_DIM, (hh + 1) * VAL_DIM)
        r0 = hh * VT_ROWS
        vext_ref[r0:r0 + VAL_DIM, :] = v_ref[0, :, hl].T
        vext_ref[r0 + VAL_DIM:r0 + VT_ROWS, :] = ones_row
        for i in range(s_len // ATTN_TQ):
            lo = i * ATTN_TQ
            qq = _split_components(q_ref[0, lo:lo + ATTN_TQ, hl])
            t = jnp.where(keep, _dot_nt(k_ref[0, lo:lo + ATTN_TQ, hl], qq), -jnp.inf)
            acc = _dot(vext_ref[r0:r0 + VT_ROWS, lo:lo + ATTN_TQ], jnp.exp2(t).astype(BF16))
            if i > 0:
                t = _dot_nt(k_ref[0, :lo, hl], qq)
                acc = acc + _dot(vext_ref[r0:r0 + VT_ROWS, :lo], jnp.exp2(t).astype(BF16))
            o2 = acc[:VAL_DIM] * (1.0 / acc[VAL_DIM:VAL_DIM + 1])
            o = o2[:, :ATTN_TQ] - lam * o2[:, ATTN_TQ:]
            ms = jnp.mean(o * o, axis=0, keepdims=True)
            y = o * lax.rsqrt(ms + EPS) * gain_t
            o_ref[0, lo:lo + ATTN_TQ, hl] = y.T.astype(BF16)


def _diff_attn(q, k, v, lam_p, subln_g, lambda_init, logit_bound):
    b, s, _ = q.shape
    out_shape = jax.ShapeDtypeStruct((b, s, TOK_W), BF16)

    def bounded(q, k, v):
        width = ATTN_HEADS_PER_STEP * VAL_DIM
        head = pl.BlockSpec((1, s, width), lambda bi, h: (bi, 0, h))
        return pl.pallas_call(
            functools.partial(_diff_attn_bounded_kernel, lambda_init=lambda_init),
            grid=(b, DIFF_HEADS // ATTN_HEADS_PER_STEP),
            in_specs=[_const_spec(lam_p.shape), _const_spec((VAL_DIM, 1)), head, head, head],
            out_specs=head,
            out_shape=out_shape,
            scratch_shapes=[pltpu.VMEM((ATTN_HEADS_PER_STEP * VT_ROWS, s), BF16)],
            compiler_params=pltpu.CompilerParams(
                dimension_semantics=("arbitrary", "arbitrary"), vmem_limit_bytes=VMEM_LIMIT),
            name="diff_attn_bounded",
        )(lam_p, subln_g.reshape(VAL_DIM, 1), q, k, v)

    def online(q, k, v):
        head = pl.BlockSpec((1, s, VAL_DIM), lambda bi, h, i: (bi, 0, h))
        blk = pl.BlockSpec((1, ATTN_TQ, VAL_DIM), lambda bi, h, i: (bi, i, h))
        return pl.pallas_call(
            functools.partial(_diff_attn_online_kernel, lambda_init=lambda_init),
            grid=(b, DIFF_HEADS, s // ATTN_TQ),
            in_specs=[_const_spec(lam_p.shape), _const_spec((1, VAL_DIM)), blk, head, head],
            out_specs=blk,
            out_shape=out_shape,
            compiler_params=pltpu.CompilerParams(
                dimension_semantics=("arbitrary", "arbitrary", "arbitrary"),
                vmem_limit_bytes=VMEM_LIMIT),
            name="diff_attn_online",
        )(lam_p, subln_g, q, k, v)

    return lax.cond(logit_bound <= FAST_MAX_LOG2, bounded, online, q, k, v)


def _tile_gain(g, reps):
    return jnp.tile(g.reshape(1, -1), (1, reps))


def kernel(x, mem, ffn_norm, ffn_w_in, ffn_w_out, mix_norm, mem_norm, w_mem_kv,
           memq_norm, memk_norm, w_out, a_w_in, a_v_norm, a_w_s, a_b_s,
           b_w_in, b_q_norm, b_k_norm, b_lambda, b_subln):
    batch, seq, d = x.shape
    depth = ffn_norm.shape[0]
    x2d = x.reshape(batch * seq, d)

    n_steps = (batch * seq) // TM
    ffn_order = [(i, half) for i in range(depth) for half in range(2)]
    ffn_gain = ffn_norm.reshape(depth, 2, 1, d)

    gk_mem = jnp.tile(memk_norm.reshape(depth, 1, HEAD_DIM), (1, 1, MEM_HEADS))
    kmem, vmem, w_in_bf, w_out_bf = _memkv(
        mem, mem_norm.reshape(depth, 1, d), w_mem_kv.astype(BF16), gk_mem,
        lambda steps, flat: _ffn_weight_riders(ffn_w_in, ffn_w_out, ffn_order[0], steps, flat))
    w_mix_out = w_out.astype(BF16)

    def ffn(x2d, pos, w_in_bf, w_out_bf):
        riders = []
        if pos + 1 < len(ffn_order):
            riders = _ffn_weight_riders(ffn_w_in, ffn_w_out, ffn_order[pos + 1], n_steps,
                                        lambda i: i)
        res = _ffn(x2d, _layer_const(ffn_gain, ffn_order[pos]), w_in_bf, w_out_bf, riders)
        return res if riders else (res[0], None, None)

    for i in range(depth):
        j = i // N_MIXERS
        g_mix = _const(mix_norm[i].reshape(1, d))
        gq_mem = _const(_tile_gain(memq_norm[i], MEM_HEADS))
        k_i, v_i = _mem_spec(kmem, i, seq), _mem_spec(vmem, i, seq)
        w_o = _layer_const(w_mix_out, (i,))
        x2d, w_in_bf, w_out_bf = ffn(x2d, 2 * i, w_in_bf, w_out_bf)
        if i % N_MIXERS == 0:
            mix = [g_mix, _const(a_w_in[j].astype(BF16)), _const(a_v_norm[j].reshape(1, TOK_W)),
                   _const(a_w_s[j]), _const(a_b_s[j].T), gq_mem]
            x2d = _gmlp_mix(x2d, mix, k_i, v_i, w_o)
        else:
            lambda_init = 0.8 - 0.6 * math.exp(-0.3 * i)
            mix = [g_mix, _const(b_w_in[j].astype(BF16)),
                   _const(_tile_gain(b_q_norm[j], 2 * DIFF_HEADS)),
                   _const(_tile_gain(b_k_norm[j], 2 * DIFF_HEADS)), gq_mem]
            q, k, v, qm = _diff_inproj(x2d, mix)
            logit_bound = (1.01 * HEAD_DIM * ATTN_SCALE * LOG2E
                           * jnp.max(jnp.abs(b_q_norm[j])) * jnp.max(jnp.abs(b_k_norm[j])))
            tok = _diff_attn(q.reshape(batch, seq, TOK_W), k.reshape(batch, seq, TOK_W),
                             v.reshape(batch, seq, TOK_W), b_lambda[j],
                             b_subln[j].reshape(1, VAL_DIM), lambda_init, logit_bound)
            x2d = _diff_outproj(x2d, tok.reshape(batch * seq, TOK_W), qm, k_i, v_i, w_o)
        x2d, w_in_bf, w_out_bf = ffn(x2d, 2 * i + 1, w_in_bf, w_out_bf)
    return x2d.reshape(batch, seq, d)
```

```python
import functools
import math

import jax
import jax.numpy as jnp
from jax import lax
from jax.experimental import pallas as pl
from jax.experimental.pallas import tpu as pltpu

D_MODEL = 1024
HEAD_DIM = 64
MEM_HEADS = 4
MEM_W = MEM_HEADS * HEAD_DIM
TOK_W = D_MODEL - MEM_W
CHUNK = 128
GROUP_DIM = 128
GROUPS = TOK_W // GROUP_DIM
DIFF_HEADS = TOK_W // (2 * HEAD_DIM)
VAL_DIM = 2 * HEAD_DIM
N_MIXERS = 2
EPS = 1e-6
ATTN_SCALE = HEAD_DIM ** -0.5
LOG2E = math.log2(math.e)
FAST_MAX_LOG2 = 60.0

LANES = 128
NORM_SLAB = 256

SUB = 256
TM = 1024
BF16_ROWS = 16
ATTN_TQ = 256
ATTN_TK = 256
ATTN_HEADS_PER_STEP = 3
VT_ROWS = VAL_DIM + BF16_ROWS
VMEM_LIMIT = 56 * 1024 * 1024

F32 = jnp.float32
BF16 = jnp.bfloat16


def _const_spec(shape):
    nd = len(shape)
    return pl.BlockSpec(shape, lambda *_: (0,) * nd, pipeline_mode=pl.Buffered(1))


def _rms(x, g):
    ms = jnp.mean(x * x, axis=-1, keepdims=True)
    return x * lax.rsqrt(ms + EPS) * g


def _dot(a, b):
    return jnp.dot(a, b, preferred_element_type=F32)


def _dot_nt(a, b):
    return lax.dot_general(a, b, (((1,), (1,)), ((), ())), preferred_element_type=F32)


def _block_diag_ones(width, group):
    r = lax.broadcasted_iota(jnp.int32, (width, width), 0) // group
    c = lax.broadcasted_iota(jnp.int32, (width, width), 1) // group
    return jnp.where(r == c, 1.0, 0.0).astype(BF16)


def _group_norm64(x, g):
    low = lax.broadcasted_iota(jnp.int32, (1, LANES), 1) < HEAD_DIM
    outs = []
    for j in range(x.shape[-1] // LANES):
        xs = x[:, j * LANES:(j + 1) * LANES]
        sq = xs * xs
        s_lo = jnp.sum(jnp.where(low, sq, 0.0), axis=-1, keepdims=True)
        s_hi = jnp.sum(jnp.where(low, 0.0, sq), axis=-1, keepdims=True)
        r_lo = lax.rsqrt(s_lo * (1.0 / HEAD_DIM) + EPS)
        r_hi = lax.rsqrt(s_hi * (1.0 / HEAD_DIM) + EPS)
        outs.append(xs * jnp.where(low, r_lo, r_hi))
    return jnp.concatenate(outs, axis=-1) * g


def _skewed_trace(stages, tiles):
    states = [{} for _ in tiles]
    for t in range(len(tiles) + len(stages) - 1):
        for j in range(len(tiles)):
            if 0 <= t - j < len(stages):
                stages[t - j](states[j], tiles[j])


def _gelu_exact(x):
    return 0.5 * x * (1.0 + lax.erf(x * math.sqrt(0.5)))


def _read(ref):
    return lambda st, r0: ref[r0:r0 + SUB, :]


def _write(ref):
    def write(st, r0, val):
        ref[r0:r0 + SUB, :] = val
    return write


def _ffn_stages(read_x, write_out, g_ref, win_ref, wout_ref):
    d_ff = wout_ref.shape[0]

    def up_proj(st, r0):
        st["ffn_x"] = read_x(st, r0)
        h = _rms(st["ffn_x"], g_ref[...]).astype(BF16)
        st["gate"] = _dot(h, win_ref[:, :d_ff])
        st["up"] = _dot(h, win_ref[:, d_ff:])

    def activate(st, r0):
        gate = st.pop("gate")
        st["act"] = (gate * jax.nn.sigmoid(gate) * st.pop("up")).astype(BF16)

    def down_proj(st, r0):
        write_out(st, r0, st.pop("ffn_x") + 0.5 * _dot(st.pop("act"), wout_ref[...]))

    return [up_proj, activate, down_proj]


def _mem_attention(qn, k_ref, v_ref):
    k, v = k_ref[0], v_ref[0]
    lane = lax.broadcasted_iota(jnp.int32, (1, MEM_W), 1)
    out = jnp.zeros((qn.shape[0], MEM_W), F32)
    for h in range(MEM_HEADS):
        sel = (lane >= h * HEAD_DIM) & (lane < (h + 1) * HEAD_DIM)
        qh = jnp.where(sel, qn, jnp.zeros_like(qn))
        s = _dot_nt(qh, k)
        p = jnp.exp(s - jnp.max(s, axis=-1, keepdims=True))
        l = jnp.sum(p, axis=-1, keepdims=True)
        vh = jnp.where(sel, v, jnp.zeros_like(v))
        out = out + _dot(p.astype(BF16), vh) / l
    return out


def _gmlp_stages(read_x, write_out, g_ref, win_ref, vg_ref, ws_ref, bst_ref, gq_ref,
                 k_ref, v_ref, wout_ref):
    n_chunks = SUB // CHUNK
    row = lax.broadcasted_iota(jnp.int32, (CHUNK, CHUNK), 0)
    col = lax.broadcasted_iota(jnp.int32, (CHUNK, CHUNK), 1)
    ws = [jnp.where(col <= row, ws_ref[g], 0.0).astype(BF16) for g in range(GROUPS)]
    bst = bst_ref[...]

    def in_proj(st, r0):
        st["mix_x"] = read_x(st, r0)
        h = _rms(st["mix_x"], g_ref[...]).astype(BF16)
        st["z"] = _dot(h, win_ref[...])

    def activate(st, r0):
        z = st.pop("z")
        st["u"] = _gelu_exact(z[:, :TOK_W])
        vact = _gelu_exact(z[:, TOK_W:2 * TOK_W])
        st["vn"] = [_rms(vact[:, g * GROUP_DIM:(g + 1) * GROUP_DIM],
                         vg_ref[:, g * GROUP_DIM:(g + 1) * GROUP_DIM]).astype(BF16)
                    for g in range(GROUPS)]
        st["qn"] = _group_norm64(z[:, 2 * TOK_W:], gq_ref[...]).astype(BF16)

    def spatial(st, r0):
        u, vn = st.pop("u"), st.pop("vn")
        tok_cols = []
        for g in range(GROUPS):
            vg = jnp.concatenate([vn[g][n * CHUNK:(n + 1) * CHUNK] for n in range(n_chunks)],
                                 axis=1)
            mixed = _dot(ws[g], vg) + bst[:, g:g + 1]
            mixed = jnp.concatenate(
                [mixed[:, n * GROUP_DIM:(n + 1) * GROUP_DIM] for n in range(n_chunks)], axis=0)
            tok_cols.append(u[:, g * GROUP_DIM:(g + 1) * GROUP_DIM] * mixed)
        st["tok"] = jnp.concatenate(tok_cols, axis=-1).astype(BF16)

    def mem_attn(st, r0):
        st["mo"] = _mem_attention(st.pop("qn"), k_ref, v_ref).astype(BF16)

    def out_proj(st, r0):
        y = _dot(st.pop("tok"), wout_ref[:TOK_W, :]) + _dot(st.pop("mo"), wout_ref[TOK_W:, :])
        write_out(st, r0, st.pop("mix_x") + y)

    return [in_proj, activate, spatial, mem_attn, out_proj]


def _diff_inproj_stages(read_x, g_ref, win_ref, gq_ref, gk_ref, gqm_ref,
                        q_ref, k_ref, v_ref, qm_ref):
    def in_proj(st, r0):
        h = _rms(read_x(st, r0), g_ref[...]).astype(BF16)
        st["z"] = _dot(h, win_ref[...])

    def normalise(st, r0):
        rs = slice(r0, r0 + SUB)
        z = st.pop("z")
        q = _group_norm64(z[:, :TOK_W], gq_ref[...]) * (ATTN_SCALE * LOG2E)
        k = _group_norm64(z[:, TOK_W:2 * TOK_W], gk_ref[...])
        q_ref[rs, :] = q.astype(BF16)
        k_ref[rs, :] = k.astype(BF16)
        v_ref[rs, :] = z[:, 2 * TOK_W:3 * TOK_W].astype(BF16)
        qm_ref[rs, :] = _group_norm64(z[:, 3 * TOK_W:], gqm_ref[...]).astype(BF16)

    return [in_proj, normalise]


def _diff_outproj_stages(read_x, write_out, tok_ref, qm_ref, k_ref, v_ref, wout_ref):
    def mem_attn(st, r0):
        st["mo"] = _mem_attention(qm_ref[r0:r0 + SUB, :], k_ref, v_ref).astype(BF16)

    def out_proj(st, r0):
        y = (_dot(tok_ref[r0:r0 + SUB, :], wout_ref[:TOK_W, :])
             + _dot(st.pop("mo"), wout_ref[TOK_W:, :]))
        write_out(st, r0, read_x(st, r0) + y)

    return [mem_attn, out_proj]


def _run_stages(stages, rows):
    _skewed_trace(stages, list(range(0, rows, SUB)))


def _ffn_kernel(x_ref, fg_ref, fwin_ref, fwout_ref, o_ref):
    _run_stages(_ffn_stages(_read(x_ref), _write(o_ref), fg_ref, fwin_ref, fwout_ref),
                x_ref.shape[0])


def _gmlp_kernel(x_ref, g_ref, win_ref, vg_ref, ws_ref, bst_ref, gq_ref, k_ref, v_ref,
                 wout_ref, o_ref):
    _run_stages(_gmlp_stages(_read(x_ref), _write(o_ref), g_ref, win_ref, vg_ref, ws_ref,
                             bst_ref, gq_ref, k_ref, v_ref, wout_ref), x_ref.shape[0])


def _diff_inproj_kernel(x_ref, g_ref, win_ref, gq_ref, gk_ref, gqm_ref,
                        q_ref, k_ref, v_ref, qm_ref):
    _run_stages(_diff_inproj_stages(_read(x_ref), g_ref, win_ref, gq_ref, gk_ref, gqm_ref,
                                    q_ref, k_ref, v_ref, qm_ref), x_ref.shape[0])


def _diff_outproj_kernel(x_ref, tok_ref, qm_ref, k_ref, v_ref, wout_ref, o_ref):
    _run_stages(_diff_outproj_stages(_read(x_ref), _write(o_ref), tok_ref, qm_ref, k_ref,
                                     v_ref, wout_ref), x_ref.shape[0])


def _cast_blocks(rows, n_steps):
    for blocks in range(n_steps, 0, -1):
        if n_steps % blocks == 0 and rows % (blocks * BF16_ROWS) == 0:
            return blocks
    raise ValueError(f"{rows} rows cannot be split into bf16 row blocks over {n_steps} steps")


def _cast_rider(stacked, lead, n_steps, flat_step):
    rows, cols = stacked.shape[len(lead):]
    blocks = _cast_blocks(rows, n_steps)
    steps_per_block = n_steps // blocks
    squeezed = (None,) * len(lead)

    def src_index(*g):
        return lead + (flat_step(*g) // steps_per_block, 0)

    def dst_index(*g):
        return (flat_step(*g) // steps_per_block, 0)

    src = pl.BlockSpec(squeezed + (rows // blocks, cols), src_index)
    dst = pl.BlockSpec((rows // blocks, cols), dst_index)
    return (stacked, src), (jax.ShapeDtypeStruct((rows, cols), BF16), dst)


def _with_riders(kernel_fn, n_in, n_out, n_riders):
    def kern(*refs):
        ins, rin = refs[:n_in], refs[n_in:n_in + n_riders]
        outs = refs[n_in + n_riders:n_in + n_riders + n_out]
        rout = refs[n_in + n_riders + n_out:n_in + 2 * n_riders + n_out]
        for src, dst in zip(rin, rout):
            dst[...] = src[...].astype(BF16)
        kernel_fn(*ins, *outs, *refs[n_in + 2 * n_riders + n_out:])
    return kern


def _ffn_weight_riders(ffn_w_in, ffn_w_out, lead, n_steps, flat_step):
    return [_cast_rider(ffn_w_in, lead, n_steps, flat_step),
            _cast_rider(ffn_w_out, lead, n_steps, flat_step)]


def _row_spec(width):
    return pl.BlockSpec((TM, width), lambda i: (i, 0))


def _layer_const(stacked, lead):
    tail = stacked.shape[len(lead):]
    spec = pl.BlockSpec((None,) * len(lead) + tail, lambda *_: lead + (0,) * len(tail),
                        pipeline_mode=pl.Buffered(1))
    return (stacked, spec)


def _mem_spec(stacked, layer, seq):
    tiles_per_row = seq // TM
    _, _, l, w = stacked.shape
    return (stacked, pl.BlockSpec((None, 1, l, w), lambda i: (layer, i // tiles_per_row, 0, 0)))


def _const(arr):
    return (arr, _const_spec(arr.shape))


def _token_call(kernel_fn, name, n, ins, outs, riders=()):
    rin = [r[0] for r in riders]
    rout = [r[1] for r in riders]
    if riders:
        kernel_fn = _with_riders(kernel_fn, len(ins), len(outs), len(riders))
    return pl.pallas_call(
        kernel_fn,
        grid=(n // TM,),
        in_specs=[spec for _, spec in ins + rin],
        out_specs=[spec for _, spec in outs + rout],
        out_shape=[shape for shape, _ in outs + rout],
        compiler_params=pltpu.CompilerParams(
            dimension_semantics=("arbitrary",), vmem_limit_bytes=VMEM_LIMIT),
        name=name,
    )(*[arr for arr, _ in ins + rin])


def _ffn(x2d, gain, w_in, w_out, riders):
    n, d = x2d.shape
    return _token_call(
        _ffn_kernel, "ffn", n,
        [(x2d, _row_spec(d)), gain, _const(w_in), _const(w_out)],
        [(jax.ShapeDtypeStruct((n, d), F32), _row_spec(d))], riders)


def _gmlp_mix(x2d, mix, kmem, vmem, w_o):
    n, d = x2d.shape
    (out,) = _token_call(
        _gmlp_kernel, "gmlp_mix", n,
        [(x2d, _row_spec(d))] + mix + [kmem, vmem, w_o],
        [(jax.ShapeDtypeStruct((n, d), F32), _row_spec(d))])
    return out


def _diff_inproj(x2d, mix):
    n, d = x2d.shape
    tok = (jax.ShapeDtypeStruct((n, TOK_W), BF16), _row_spec(TOK_W))
    return _token_call(
        _diff_inproj_kernel, "diff_inproj", n, [(x2d, _row_spec(d))] + mix,
        [tok, tok, tok, (jax.ShapeDtypeStruct((n, MEM_W), BF16), _row_spec(MEM_W))])


def _diff_outproj(x2d, tok, qm, kmem, vmem, w_o):
    n, d = x2d.shape
    (out,) = _token_call(
        _diff_outproj_kernel, "diff_outproj", n,
        [(x2d, _row_spec(d)), (tok, _row_spec(TOK_W)), (qm, _row_spec(MEM_W)), kmem, vmem, w_o],
        [(jax.ShapeDtypeStruct((n, d), F32), _row_spec(d))])
    return out


def _memkv_kernel(mem_ref, g_ref, w_ref, gk_ref, k_ref, v_ref):
    mh = _rms(mem_ref[0], g_ref[0]).astype(BF16)
    kv = _dot(mh, w_ref[0])
    k = _group_norm64(kv[:, :MEM_W], gk_ref[0]) * ATTN_SCALE
    k_ref[0, 0] = k.astype(BF16)
    v_ref[0, 0] = kv[:, MEM_W:].astype(BF16)


def _memkv(mem, mem_norm, w_kv, gk_tiled, make_riders):
    depth = w_kv.shape[0]
    b, l, d = mem.shape
    kv_out = (jax.ShapeDtypeStruct((depth, b, l, MEM_W), BF16),
              pl.BlockSpec((1, 1, l, MEM_W), lambda i, j: (i, j, 0, 0)))
    riders = make_riders(depth * b, lambda i, j: i * b + j)
    ins = [(mem, pl.BlockSpec((1, l, d), lambda i, j: (j, 0, 0))),
           (mem_norm, pl.BlockSpec((1, 1, d), lambda i, j: (i, 0, 0))),
           (w_kv, pl.BlockSpec((1, d, 2 * MEM_W), lambda i, j: (i, 0, 0))),
           (gk_tiled, pl.BlockSpec((1, 1, MEM_W), lambda i, j: (i, 0, 0)))]
    ins += [r[0] for r in riders]
    outs = [kv_out, kv_out] + [r[1] for r in riders]
    return pl.pallas_call(
        _with_riders(_memkv_kernel, 4, 2, len(riders)),
        grid=(depth, b),
        in_specs=[spec for _, spec in ins],
        out_specs=[spec for _, spec in outs],
        out_shape=[shape for shape, _ in outs],
        compiler_params=pltpu.CompilerParams(
            dimension_semantics=("arbitrary", "arbitrary"), vmem_limit_bytes=VMEM_LIMIT),
        name="memkv",
    )(*[arr for arr, _ in ins])


def _split_components(q):
    lane = lax.broadcasted_iota(jnp.int32, (1, VAL_DIM), 1)
    zero = jnp.zeros_like(q)
    return jnp.concatenate([jnp.where(lane < HEAD_DIM, q, zero),
                            jnp.where(lane >= HEAD_DIM, q, zero)], axis=0)


def _causal_keep(rows, cols):
    r = lax.broadcasted_iota(jnp.int32, (rows, cols), 0)
    c = lax.broadcasted_iota(jnp.int32, (rows, cols), 1)
    return jnp.concatenate([c <= r, c <= r], axis=0)


def _diff_combine(o2, lam_ref, sg_ref, lambda_init):
    rows = o2.shape[0] // 2
    lp = lam_ref[...]
    lam = (jnp.exp(jnp.sum(lp[0:1] * lp[1:2], axis=-1, keepdims=True))
           - jnp.exp(jnp.sum(lp[2:3] * lp[3:4], axis=-1, keepdims=True)) + lambda_init)
    o = o2[:rows] - lam * o2[rows:]
    return (_rms(o, sg_ref[...]) * (1.0 - lambda_init)).astype(BF16)


def _diff_attn_online_kernel(lam_ref, sg_ref, q_ref, k_ref, v_ref, o_ref, *, lambda_init):
    qi = pl.program_id(2)
    qq = _split_components(q_ref[0])

    def scores(j):
        rows = pl.ds(pl.multiple_of(j * ATTN_TK, ATTN_TK), ATTN_TK)
        return _dot_nt(qq, k_ref[0, rows, :]), v_ref[0, rows, :]

    s, vb = scores(qi)
    s = jnp.where(_causal_keep(ATTN_TQ, ATTN_TK), s, -jnp.inf)
    m = jnp.max(s, axis=-1, keepdims=True)
    p = jnp.exp2(s - m)
    l = jnp.sum(p, axis=-1, keepdims=True)
    acc = _dot(p.astype(BF16), vb)

    def body(j, carry):
        m, l, acc = carry
        s, vb = scores(j)
        m_new = jnp.maximum(m, jnp.max(s, axis=-1, keepdims=True))
        alpha = jnp.exp2(m - m_new)
        p = jnp.exp2(s - m_new)
        l = alpha * l + jnp.sum(p, axis=-1, keepdims=True)
        acc = alpha * acc + _dot(p.astype(BF16), vb)
        return m_new, l, acc

    m, l, acc = lax.fori_loop(0, qi, body, (m, l, acc))
    o_ref[0] = _diff_combine(acc / l, lam_ref, sg_ref, lambda_init)


def _diff_attn_bounded_kernel(lam_ref, sg_ref, q_ref, k_ref, v_ref, o_ref, vext_ref,
                              *, lambda_init):
    s_len = q_ref.shape[1]
    row = lax.broadcasted_iota(jnp.int32, (VT_ROWS - VAL_DIM, s_len), 0)
    ones_row = jnp.where(row == 0, 1.0, 0.0).astype(BF16)
    key_i = lax.broadcasted_iota(jnp.int32, (ATTN_TQ, ATTN_TQ), 0)
    qry_i = lax.broadcasted_iota(jnp.int32, (ATTN_TQ, ATTN_TQ), 1)
    keep = jnp.concatenate([key_i <= qry_i, key_i <= qry_i], axis=1)
    lp = lam_ref[...]
    lam = (jnp.exp(jnp.sum(lp[0:1] * lp[1:2], axis=-1, keepdims=True))
           - jnp.exp(jnp.sum(lp[2:3] * lp[3:4], axis=-1, keepdims=True)) + lambda_init)
    gain_t = sg_ref[...] * (1.0 - lambda_init)

    for hh in range(q_ref.shape[2] // VAL_DIM):
        hl = slice(hh * VAL_DIM, (hh + 1) * VAL_DIM)
        r0 = hh * VT_ROWS
        vext_ref[r0:r0 + VAL_DIM, :] = v_ref[0, :, hl].T
        vext_ref[r0 + VAL_DIM:r0 + VT_ROWS, :] = ones_row
        for i in range(s_len // ATTN_TQ):
            lo = i * ATTN_TQ
            qq = _split_components(q_ref[0, lo:lo + ATTN_TQ, hl])
            t = jnp.where(keep, _dot_nt(k_ref[0, lo:lo + ATTN_TQ, hl], qq), -jnp.inf)
            acc = _dot(vext_ref[r0:r0 + VT_ROWS, lo:lo + ATTN_TQ], jnp.exp2(t).astype(BF16))
            if i > 0:
                t = _dot_nt(k_ref[0, :lo, hl], qq)
                acc = acc + _dot(vext_ref[r0:r0 + VT_ROWS, :lo], jnp.exp2(t).astype(BF16))
            o2 = acc[:VAL_DIM] * (1.0 / acc[VAL_DIM:VAL_DIM + 1])
            o = o2[:, :ATTN_TQ] - lam * o2[:, ATTN_TQ:]
            ms = jnp.mean(o * o, axis=0, keepdims=True)
            y = o * lax.rsqrt(ms + EPS) * gain_t
            o_ref[0, lo:lo + ATTN_TQ, hl] = y.T.astype(BF16)


def _diff_attn(q, k, v, lam_p, subln_g, lambda_init, logit_bound):
    b, s, _ = q.shape
    out_shape = jax.ShapeDtypeStruct((b, s, TOK_W), BF16)

    def bounded(q, k, v):
        width = ATTN_HEADS_PER_STEP * VAL_DIM
        head = pl.BlockSpec((1, s, width), lambda bi, h: (bi, 0, h))
        return pl.pallas_call(
            functools.partial(_diff_attn_bounded_kernel, lambda_init=lambda_init),
            grid=(b, DIFF_HEADS // ATTN_HEADS_PER_STEP),
            in_specs=[_const_spec(lam_p.shape), _const_spec((VAL_DIM, 1)), head, head, head],
            out_specs=head,
            out_shape=out_shape,
            scratch_shapes=[pltpu.VMEM((ATTN_HEADS_PER_STEP * VT_ROWS, s), BF16)],
            compiler_params=pltpu.CompilerParams(
                dimension_semantics=("arbitrary", "arbitrary"), vmem_limit_bytes=VMEM_LIMIT),
            name="diff_attn_bounded",
        )(lam_p, subln_g.reshape(VAL_DIM, 1), q, k, v)

    def online(q, k, v):
        head = pl.BlockSpec((1, s, VAL_DIM), lambda bi, h, i: (bi, 0, h))
        blk = pl.BlockSpec((1, ATTN_TQ, VAL_DIM), lambda bi, h, i: (bi, i, h))
        return pl.pallas_call(
            functools.partial(_diff_attn_online_kernel, lambda_init=lambda_init),
            grid=(b, DIFF_HEADS, s // ATTN_TQ),
            in_specs=[_const_spec(lam_p.shape), _const_spec((1, VAL_DIM)), blk, head, head],
            out_specs=blk,
            out_shape=out_shape,
            compiler_params=pltpu.CompilerParams(
                dimension_semantics=("arbitrary", "arbitrary", "arbitrary"),
                vmem_limit_bytes=VMEM_LIMIT),
            name="diff_attn_online",
        )(lam_p, subln_g, q, k, v)

    return lax.cond(logit_bound <= FAST_MAX_LOG2, bounded, online, q, k, v)


def _tile_gain(g, reps):
    return jnp.tile(g.reshape(1, -1), (1, reps))


def kernel(x, mem, ffn_norm, ffn_w_in, ffn_w_out, mix_norm, mem_norm, w_mem_kv,
           memq_norm, memk_norm, w_out, a_w_in, a_v_norm, a_w_s, a_b_s,
           b_w_in, b_q_norm, b_k_norm, b_lambda, b_subln):
    batch, seq, d = x.shape
    depth = ffn_norm.shape[0]
    x2d = x.reshape(batch * seq, d)

    n_steps = (batch * seq) // TM
    ffn_order = [(i, half) for i in range(depth) for half in range(2)]
    ffn_gain = ffn_norm.reshape(depth, 2, 1, d)

    gk_mem = jnp.tile(memk_norm.reshape(depth, 1, HEAD_DIM), (1, 1, MEM_HEADS))
    kmem, vmem, w_in_bf, w_out_bf = _memkv(
        mem, mem_norm.reshape(depth, 1, d), w_mem_kv.astype(BF16), gk_mem,
        lambda steps, flat: _ffn_weight_riders(ffn_w_in, ffn_w_out, ffn_order[0], steps, flat))

    def ffn(x2d, pos, w_in_bf, w_out_bf, extra_riders=()):
        riders = []
        if pos + 1 < len(ffn_order):
            riders = _ffn_weight_riders(ffn_w_in, ffn_w_out, ffn_order[pos + 1], n_steps,
                                        lambda i: i)
        n_next = len(riders)
        res = _ffn(x2d, _layer_const(ffn_gain, ffn_order[pos]), w_in_bf, w_out_bf,
                   riders + list(extra_riders))
        nxt = res[1:1 + n_next] if n_next else [None, None]
        return res[0], nxt[0], nxt[1], res[1 + n_next:]

    for i in range(depth):
        j = i // N_MIXERS
        g_mix = _const(mix_norm[i].reshape(1, d))
        gq_mem = _const(_tile_gain(memq_norm[i], MEM_HEADS))
        k_i, v_i = _mem_spec(kmem, i, seq), _mem_spec(vmem, i, seq)
        w_mix_in = a_w_in if i % N_MIXERS == 0 else b_w_in
        mix_riders = [_cast_rider(w_mix_in, (j,), n_steps, lambda s: s),
                      _cast_rider(w_out, (i,), n_steps, lambda s: s)]
        x2d, w_in_bf, w_out_bf, (w_mix_in_bf, w_o_bf) = ffn(x2d, 2 * i, w_in_bf, w_out_bf,
                                                            mix_riders)
        w_o = _const(w_o_bf)
        if i % N_MIXERS == 0:
            mix = [g_mix, _const(w_mix_in_bf), _const(a_v_norm[j].reshape(1, TOK_W)),
                   _const(a_w_s[j]), _const(a_b_s[j].T), gq_mem]
            x2d = _gmlp_mix(x2d, mix, k_i, v_i, w_o)
        else:
            lambda_init = 0.8 - 0.6 * math.exp(-0.3 * i)
            mix = [g_mix, _const(w_mix_in_bf),
                   _const(_tile_gain(b_q_norm[j], 2 * DIFF_HEADS)),
                   _const(_tile_gain(b_k_norm[j], 2 * DIFF_HEADS)), gq_mem]
            q, k, v, qm = _diff_inproj(x2d, mix)
            logit_bound = (1.01 * HEAD_DIM * ATTN_SCALE * LOG2E
                           * jnp.max(jnp.abs(b_q_norm[j])) * jnp.max(jnp.abs(b_k_norm[j])))
            tok = _diff_attn(q.reshape(batch, seq, TOK_W), k.reshape(batch, seq, TOK_W),
                             v.reshape(batch, seq, TOK_W), b_lambda[j],
                             b_subln[j].reshape(1, VAL_DIM), lambda_init, logit_bound)
            x2d = _diff_outproj(x2d, tok.reshape(batch * seq, TOK_W), qm, k_i, v_i, w_o)
        x2d, w_in_bf, w_out_bf, _ = ffn(x2d, 2 * i + 1, w_in_bf, w_out_bf)
    return x2d.reshape(batch, seq, d)
```

```python
import functools
import math

import jax
import jax.numpy as jnp
from jax import lax
from jax.experimental import pallas as pl
from jax.experimental.pallas import tpu as pltpu

D_MODEL = 1024
HEAD_DIM = 64
MEM_HEADS = 4
MEM_W = MEM_HEADS * HEAD_DIM
TOK_W = D_MODEL - MEM_W
CHUNK = 128
GROUP_DIM = 128
GROUPS = TOK_W // GROUP_DIM
DIFF_HEADS = TOK_W // (2 * HEAD_DIM)
VAL_DIM = 2 * HEAD_DIM
N_MIXERS = 2
EPS = 1e-6
ATTN_SCALE = HEAD_DIM ** -0.5
LOG2E = math.log2(math.e)
FAST_MAX_LOG2 = 60.0

LANES = 128

SUB = 256
TM_FFN = 1024
TM_GMLP = 1024
TM_MIX = 2048
BF16_ROWS = 16
ATTN_TQ = 256
ATTN_TK = 256
ATTN_HEADS_PER_STEP = 3
HEAD_GROUPS = DIFF_HEADS // ATTN_HEADS_PER_STEP
GROUP_W = ATTN_HEADS_PER_STEP * VAL_DIM
VT_ROWS = VAL_DIM + BF16_ROWS
VMEM_LIMIT = 56 * 1024 * 1024

F32 = jnp.float32
BF16 = jnp.bfloat16


def _const_spec(shape):
    nd = len(shape)
    return pl.BlockSpec(shape, lambda *_: (0,) * nd, pipeline_mode=pl.Buffered(1))


def _rms(x, g):
    ms = jnp.mean(x * x, axis=-1, keepdims=True)
    return x * lax.rsqrt(ms + EPS) * g


def _dot(a, b):
    return jnp.dot(a, b, preferred_element_type=F32)


def _dot_nt(a, b):
    return lax.dot_general(a, b, (((1,), (1,)), ((), ())), preferred_element_type=F32)


def _group_norm64(x, g):
    low = lax.broadcasted_iota(jnp.int32, (1, LANES), 1) < HEAD_DIM
    outs = []
    for j in range(x.shape[-1] // LANES):
        xs = x[:, j * LANES:(j + 1) * LANES]
        sq = xs * xs
        s_lo = jnp.sum(jnp.where(low, sq, 0.0), axis=-1, keepdims=True)
        s_hi = jnp.sum(jnp.where(low, 0.0, sq), axis=-1, keepdims=True)
        r_lo = lax.rsqrt(s_lo * (1.0 / HEAD_DIM) + EPS)
        r_hi = lax.rsqrt(s_hi * (1.0 / HEAD_DIM) + EPS)
        outs.append(xs * jnp.where(low, r_lo, r_hi))
    return jnp.concatenate(outs, axis=-1) * g


def _skewed_trace(stages, tiles):
    states = [{} for _ in tiles]
    for t in range(len(tiles) + len(stages) - 1):
        for j in range(len(tiles)):
            if 0 <= t - j < len(stages):
                stages[t - j](states[j], tiles[j])


def _gelu_exact(x):
    return 0.5 * x * (1.0 + lax.erf(x * math.sqrt(0.5)))


def _read(ref):
    return lambda st, r0: ref[r0:r0 + SUB, :]


def _write(ref):
    def write(st, r0, val):
        ref[r0:r0 + SUB, :] = val
    return write


def _ffn_stages(read_x, write_out, g_ref, win_ref, wout_ref):
    d_ff = wout_ref.shape[0]

    def up_proj(st, r0):
        st["ffn_x"] = read_x(st, r0)
        h = _rms(st["ffn_x"], g_ref[...]).astype(BF16)
        st["gate"] = _dot(h, win_ref[:, :d_ff])
        st["up"] = _dot(h, win_ref[:, d_ff:])

    def activate(st, r0):
        gate = st.pop("gate")
        st["act"] = (gate * jax.nn.sigmoid(gate) * st.pop("up")).astype(BF16)

    def down_proj(st, r0):
        write_out(st, r0, st.pop("ffn_x") + 0.5 * _dot(st.pop("act"), wout_ref[...]))

    return [up_proj, activate, down_proj]


def _mem_attention(qn, k_ref, v_ref):
    k, v = k_ref[0], v_ref[0]
    lane = lax.broadcasted_iota(jnp.int32, (1, MEM_W), 1)
    out = jnp.zeros((qn.shape[0], MEM_W), F32)
    for h in range(MEM_HEADS):
        sel = (lane >= h * HEAD_DIM) & (lane < (h + 1) * HEAD_DIM)
        qh = jnp.where(sel, qn, jnp.zeros_like(qn))
        s = _dot_nt(qh, k)
        p = jnp.exp(s - jnp.max(s, axis=-1, keepdims=True))
        l = jnp.sum(p, axis=-1, keepdims=True)
        vh = jnp.where(sel, v, jnp.zeros_like(v))
        out = out + _dot(p.astype(BF16), vh) / l
    return out


def _gmlp_stages(read_x, write_out, g_ref, win_ref, vg_ref, ws_ref, bst_ref, gq_ref,
                 k_ref, v_ref, wout_ref):
    n_chunks = SUB // CHUNK
    row = lax.broadcasted_iota(jnp.int32, (CHUNK, CHUNK), 0)
    col = lax.broadcasted_iota(jnp.int32, (CHUNK, CHUNK), 1)
    ws = [jnp.where(col <= row, ws_ref[g], 0.0).astype(BF16) for g in range(GROUPS)]
    bst = bst_ref[...]

    def in_proj(st, r0):
        st["mix_x"] = read_x(st, r0)
        h = _rms(st["mix_x"], g_ref[...]).astype(BF16)
        st["z"] = _dot(h, win_ref[...])

    def activate(st, r0):
        z = st.pop("z")
        st["u"] = _gelu_exact(z[:, :TOK_W])
        vact = _gelu_exact(z[:, TOK_W:2 * TOK_W])
        st["vn"] = [_rms(vact[:, g * GROUP_DIM:(g + 1) * GROUP_DIM],
                         vg_ref[:, g * GROUP_DIM:(g + 1) * GROUP_DIM]).astype(BF16)
                    for g in range(GROUPS)]
        st["qn"] = _group_norm64(z[:, 2 * TOK_W:], gq_ref[...]).astype(BF16)

    def spatial(st, r0):
        u, vn = st.pop("u"), st.pop("vn")
        tok_cols = []
        for g in range(GROUPS):
            vg = jnp.concatenate([vn[g][n * CHUNK:(n + 1) * CHUNK] for n in range(n_chunks)],
                                 axis=1)
            mixed = _dot(ws[g], vg) + bst[:, g:g + 1]
            mixed = jnp.concatenate(
                [mixed[:, n * GROUP_DIM:(n + 1) * GROUP_DIM] for n in range(n_chunks)], axis=0)
            tok_cols.append(u[:, g * GROUP_DIM:(g + 1) * GROUP_DIM] * mixed)
        st["tok"] = jnp.concatenate(tok_cols, axis=-1).astype(BF16)

    def mem_attn(st, r0):
        st["mo"] = _mem_attention(st.pop("qn"), k_ref, v_ref).astype(BF16)

    def out_proj(st, r0):
        y = _dot(st.pop("tok"), wout_ref[:TOK_W, :]) + _dot(st.pop("mo"), wout_ref[TOK_W:, :])
        write_out(st, r0, st.pop("mix_x") + y)

    return [in_proj, activate, spatial, mem_attn, out_proj]


def _diff_inproj_stages(read_x, g_ref, win_ref, gq_ref, gk_ref, gqm_ref,
                        q_ref, k_ref, v_ref, qm_ref):
    def in_proj(st, r0):
        h = _rms(read_x(st, r0), g_ref[...]).astype(BF16)
        st["z"] = _dot(h, win_ref[...])

    def normalise(st, r0):
        rs = slice(r0, r0 + SUB)
        z = st.pop("z")
        q = _group_norm64(z[:, :TOK_W], gq_ref[...]) * (ATTN_SCALE * LOG2E)
        k = _group_norm64(z[:, TOK_W:2 * TOK_W], gk_ref[...])
        v = z[:, 2 * TOK_W:3 * TOK_W]
        for g in range(HEAD_GROUPS):
            cs = slice(g * GROUP_W, (g + 1) * GROUP_W)
            q_ref[g, rs, :] = q[:, cs].astype(BF16)
            k_ref[g, rs, :] = k[:, cs].astype(BF16)
            v_ref[g, rs, :] = v[:, cs].astype(BF16)
        qm_ref[rs, :] = _group_norm64(z[:, 3 * TOK_W:], gqm_ref[...]).astype(BF16)

    return [in_proj, normalise]


def _diff_outproj_stages(read_x, write_out, tok_ref, qm_ref, k_ref, v_ref, wout_ref):
    def mem_attn(st, r0):
        st["mo"] = _mem_attention(qm_ref[r0:r0 + SUB, :], k_ref, v_ref).astype(BF16)

    def out_proj(st, r0):
        y = _dot(st.pop("mo"), wout_ref[TOK_W:, :])
        for g in range(HEAD_GROUPS):
            y = y + _dot(tok_ref[g, r0:r0 + SUB, :], wout_ref[g * GROUP_W:(g + 1) * GROUP_W, :])
        write_out(st, r0, read_x(st, r0) + y)

    return [mem_attn, out_proj]


def _run_stages(stages, rows):
    _skewed_trace(stages, list(range(0, rows, SUB)))


def _ffn_kernel(x_ref, fg_ref, fwin_ref, fwout_ref, o_ref):
    _run_stages(_ffn_stages(_read(x_ref), _write(o_ref), fg_ref, fwin_ref, fwout_ref),
                x_ref.shape[0])


def _gmlp_kernel(x_ref, g_ref, win_ref, vg_ref, ws_ref, bst_ref, gq_ref, k_ref, v_ref,
                 wout_ref, o_ref):
    _run_stages(_gmlp_stages(_read(x_ref), _write(o_ref), g_ref, win_ref, vg_ref, ws_ref,
                             bst_ref, gq_ref, k_ref, v_ref, wout_ref), x_ref.shape[0])


def _diff_inproj_kernel(x_ref, g_ref, win_ref, gq_ref, gk_ref, gqm_ref,
                        q_ref, k_ref, v_ref, qm_ref):
    _run_stages(_diff_inproj_stages(_read(x_ref), g_ref, win_ref, gq_ref, gk_ref, gqm_ref,
                                    q_ref, k_ref, v_ref, qm_ref), x_ref.shape[0])


def _diff_outproj_kernel(x_ref, tok_ref, qm_ref, k_ref, v_ref, wout_ref, o_ref):
    _run_stages(_diff_outproj_stages(_read(x_ref), _write(o_ref), tok_ref, qm_ref, k_ref,
                                     v_ref, wout_ref), x_ref.shape[0])


def _cast_blocks(rows, n_steps):
    for blocks in range(n_steps, 0, -1):
        if n_steps % blocks == 0 and rows % (blocks * BF16_ROWS) == 0:
            return blocks
    raise ValueError(f"{rows} rows cannot be split into bf16 row blocks over {n_steps} steps")


def _cast_rider(stacked, lead, n_steps, flat_step):
    rows, cols = stacked.shape[len(lead):]
    blocks = _cast_blocks(rows, n_steps)
    steps_per_block = n_steps // blocks
    squeezed = (None,) * len(lead)

    def src_index(*g):
        return lead + (flat_step(*g) // steps_per_block, 0)

    def dst_index(*g):
        return (flat_step(*g) // steps_per_block, 0)

    src = pl.BlockSpec(squeezed + (rows // blocks, cols), src_index)
    dst = pl.BlockSpec((rows // blocks, cols), dst_index)
    return (stacked, src), (jax.ShapeDtypeStruct((rows, cols), BF16), dst)


def _with_riders(kernel_fn, n_in, n_out, n_riders):
    def kern(*refs):
        ins, rin = refs[:n_in], refs[n_in:n_in + n_riders]
        outs = refs[n_in + n_riders:n_in + n_riders + n_out]
        rout = refs[n_in + n_riders + n_out:n_in + 2 * n_riders + n_out]
        for src, dst in zip(rin, rout):
            dst[...] = src[...].astype(BF16)
        kernel_fn(*ins, *outs, *refs[n_in + 2 * n_riders + n_out:])
    return kern


def _ffn_weight_riders(ffn_w_in, ffn_w_out, lead, n_steps, flat_step):
    return [_cast_rider(ffn_w_in, lead, n_steps, flat_step),
            _cast_rider(ffn_w_out, lead, n_steps, flat_step)]


def _row_spec(tm, width):
    return pl.BlockSpec((tm, width), lambda i: (i, 0))


def _group_row_spec(tm):
    return pl.BlockSpec((HEAD_GROUPS, tm, GROUP_W), lambda i: (0, i, 0))


def _layer_const(stacked, lead):
    tail = stacked.shape[len(lead):]
    spec = pl.BlockSpec((None,) * len(lead) + tail, lambda *_: lead + (0,) * len(tail),
                        pipeline_mode=pl.Buffered(1))
    return (stacked, spec)


def _mem_spec(stacked, layer, seq, tm):
    tiles_per_row = seq // tm
    _, _, l, w = stacked.shape
    return (stacked, pl.BlockSpec((None, 1, l, w), lambda i: (layer, i // tiles_per_row, 0, 0)))


def _const(arr):
    return (arr, _const_spec(arr.shape))


def _token_call(kernel_fn, name, n, tm, ins, outs, riders=()):
    rin = [r[0] for r in riders]
    rout = [r[1] for r in riders]
    if riders:
        kernel_fn = _with_riders(kernel_fn, len(ins), len(outs), len(riders))
    return pl.pallas_call(
        kernel_fn,
        grid=(n // tm,),
        in_specs=[spec for _, spec in ins + rin],
        out_specs=[spec for _, spec in outs + rout],
        out_shape=[shape for shape, _ in outs + rout],
        compiler_params=pltpu.CompilerParams(
            dimension_semantics=("arbitrary",), vmem_limit_bytes=VMEM_LIMIT),
        name=name,
    )(*[arr for arr, _ in ins + rin])


def _ffn(x2d, gain, w_in, w_out, riders):
    n, d = x2d.shape
    return _token_call(
        _ffn_kernel, "ffn", n, TM_FFN,
        [(x2d, _row_spec(TM_FFN, d)), gain, _const(w_in), _const(w_out)],
        [(jax.ShapeDtypeStruct((n, d), F32), _row_spec(TM_FFN, d))], riders)


def _gmlp_mix(x2d, mix, kmem, vmem, w_o):
    n, d = x2d.shape
    (out,) = _token_call(
        _gmlp_kernel, "gmlp_mix", n, TM_GMLP,
        [(x2d, _row_spec(TM_GMLP, d))] + mix + [kmem, vmem, w_o],
        [(jax.ShapeDtypeStruct((n, d), F32), _row_spec(TM_GMLP, d))])
    return out


def _diff_inproj(x2d, mix):
    n, d = x2d.shape
    grouped = (jax.ShapeDtypeStruct((HEAD_GROUPS, n, GROUP_W), BF16), _group_row_spec(TM_MIX))
    return _token_call(
        _diff_inproj_kernel, "diff_inproj", n, TM_MIX, [(x2d, _row_spec(TM_MIX, d))] + mix,
        [grouped, grouped, grouped,
         (jax.ShapeDtypeStruct((n, MEM_W), BF16), _row_spec(TM_MIX, MEM_W))])


def _diff_outproj(x2d, tok, qm, kmem, vmem, w_o):
    n, d = x2d.shape
    (out,) = _token_call(
        _diff_outproj_kernel, "diff_outproj", n, TM_MIX,
        [(x2d, _row_spec(TM_MIX, d)), (tok, _group_row_spec(TM_MIX)),
         (qm, _row_spec(TM_MIX, MEM_W)), kmem, vmem, w_o],
        [(jax.ShapeDtypeStruct((n, d), F32), _row_spec(TM_MIX, d))])
    return out


def _memkv_kernel(mem_ref, g_ref, w_ref, gk_ref, k_ref, v_ref):
    mh = _rms(mem_ref[0], g_ref[0]).astype(BF16)
    kv = _dot(mh, w_ref[0])
    k = _group_norm64(kv[:, :MEM_W], gk_ref[0]) * ATTN_SCALE
    k_ref[0, 0] = k.astype(BF16)
    v_ref[0, 0] = kv[:, MEM_W:].astype(BF16)


def _memkv(mem, mem_norm, w_kv, gk_tiled, make_riders):
    depth = w_kv.shape[0]
    b, l, d = mem.shape
    kv_out = (jax.ShapeDtypeStruct((depth, b, l, MEM_W), BF16),
              pl.BlockSpec((1, 1, l, MEM_W), lambda i, j: (i, j, 0, 0)))
    riders = make_riders(depth * b, lambda i, j: i * b + j)
    ins = [(mem, pl.BlockSpec((1, l, d), lambda i, j: (j, 0, 0))),
           (mem_norm, pl.BlockSpec((1, 1, d), lambda i, j: (i, 0, 0))),
           (w_kv, pl.BlockSpec((1, d, 2 * MEM_W), lambda i, j: (i, 0, 0))),
           (gk_tiled, pl.BlockSpec((1, 1, MEM_W), lambda i, j: (i, 0, 0)))]
    ins += [r[0] for r in riders]
    outs = [kv_out, kv_out] + [r[1] for r in riders]
    return pl.pallas_call(
        _with_riders(_memkv_kernel, 4, 2, len(riders)),
        grid=(depth, b),
        in_specs=[spec for _, spec in ins],
        out_specs=[spec for _, spec in outs],
        out_shape=[shape for shape, _ in outs],
        compiler_params=pltpu.CompilerParams(
            dimension_semantics=("arbitrary", "arbitrary"), vmem_limit_bytes=VMEM_LIMIT),
        name="memkv",
    )(*[arr for arr, _ in ins])


def _split_components(q):
    lane = lax.broadcasted_iota(jnp.int32, (1, VAL_DIM), 1)
    zero = jnp.zeros_like(q)
    return jnp.concatenate([jnp.where(lane < HEAD_DIM, q, zero),
                            jnp.where(lane >= HEAD_DIM, q, zero)], axis=0)


def _causal_keep(rows, cols):
    r = lax.broadcasted_iota(jnp.int32, (rows, cols), 0)
    c = lax.broadcasted_iota(jnp.int32, (rows, cols), 1)
    return jnp.concatenate([c <= r, c <= r], axis=0)


def _diff_combine(o2, lam_ref, sg_ref, lambda_init):
    rows = o2.shape[0] // 2
    lp = lam_ref[...]
    lam = (jnp.exp(jnp.sum(lp[0:1] * lp[1:2], axis=-1, keepdims=True))
           - jnp.exp(jnp.sum(lp[2:3] * lp[3:4], axis=-1, keepdims=True)) + lambda_init)
    o = o2[:rows] - lam * o2[rows:]
    return (_rms(o, sg_ref[...]) * (1.0 - lambda_init)).astype(BF16)


def _diff_attn_online_kernel(lam_ref, sg_ref, q_ref, k_ref, v_ref, o_ref, *, lambda_init):
    qi = pl.program_id(2)
    qq = _split_components(q_ref[0])

    def scores(j):
        rows = pl.ds(pl.multiple_of(j * ATTN_TK, ATTN_TK), ATTN_TK)
        return _dot_nt(qq, k_ref[0, rows, :]), v_ref[0, rows, :]

    s, vb = scores(qi)
    s = jnp.where(_causal_keep(ATTN_TQ, ATTN_TK), s, -jnp.inf)
    m = jnp.max(s, axis=-1, keepdims=True)
    p = jnp.exp2(s - m)
    l = jnp.sum(p, axis=-1, keepdims=True)
    acc = _dot(p.astype(BF16), vb)

    def body(j, carry):
        m, l, acc = carry
        s, vb = scores(j)
        m_new = jnp.maximum(m, jnp.max(s, axis=-1, keepdims=True))
        alpha = jnp.exp2(m - m_new)
        p = jnp.exp2(s - m_new)
        l = alpha * l + jnp.sum(p, axis=-1, keepdims=True)
        acc = alpha * acc + _dot(p.astype(BF16), vb)
        return m_new, l, acc

    m, l, acc = lax.fori_loop(0, qi, body, (m, l, acc))
    o_ref[0] = _diff_combine(acc / l, lam_ref, sg_ref, lambda_init)


def _diff_attn_bounded_kernel(lam_ref, sg_ref, q_ref, k_ref, v_ref, o_ref, vext_ref,
                              *, lambda_init):
    s_len = q_ref.shape[1]
    row = lax.broadcasted_iota(jnp.int32, (VT_ROWS - VAL_DIM, s_len), 0)
    ones_row = jnp.where(row == 0, 1.0, 0.0).astype(BF16)
    key_i = lax.broadcasted_iota(jnp.int32, (ATTN_TQ, ATTN_TQ), 0)
    qry_i = lax.broadcasted_iota(jnp.int32, (ATTN_TQ, ATTN_TQ), 1)
    keep = jnp.concatenate([key_i <= qry_i, key_i <= qry_i], axis=1)
    lp = lam_ref[...]
    lam = (jnp.exp(jnp.sum(lp[0:1] * lp[1:2], axis=-1, keepdims=True))
           - jnp.exp(jnp.sum(lp[2:3] * lp[3:4], axis=-1, keepdims=True)) + lambda_init)
    gain_t = sg_ref[...] * (1.0 - lambda_init)

    for hh in range(q_ref.shape[2] // VAL_DIM):
        hl = slice(hh * VAL_DIM, (hh + 1) * VAL_DIM)
        r0 = hh * VT_ROWS
        vext_ref[r0:r0 + VAL_DIM, :] = v_ref[0, :, hl].T
        vext_ref[r0 + VAL_DIM:r0 + VT_ROWS, :] = ones_row
        for i in range(s_len // ATTN_TQ):
            lo = i * ATTN_TQ
            qq = _split_components(q_ref[0, lo:lo + ATTN_TQ, hl])
            t = jnp.where(keep, _dot_nt(k_ref[0, lo:lo + ATTN_TQ, hl], qq), -jnp.inf)
            acc = _dot(vext_ref[r0:r0 + VT_ROWS, lo:lo + ATTN_TQ], jnp.exp2(t).astype(BF16))
            if i > 0:
                t = _dot_nt(k_ref[0, :lo, hl], qq)
                acc = acc + _dot(vext_ref[r0:r0 + VT_ROWS, :lo], jnp.exp2(t).astype(BF16))
            o2 = acc[:VAL_DIM] * (1.0 / acc[VAL_DIM:VAL_DIM + 1])
            o = o2[:, :ATTN_TQ] - lam * o2[:, ATTN_TQ:]
            ms = jnp.mean(o * o, axis=0, keepdims=True)
            y = o * lax.rsqrt(ms + EPS) * gain_t
            o_ref[0, lo:lo + ATTN_TQ, hl] = y.T.astype(BF16)


def _diff_attn(q, k, v, lam_p, subln_g, lambda_init, logit_bound):
    _, b, s, _ = q.shape
    out_shape = jax.ShapeDtypeStruct(q.shape, BF16)

    def bounded(q, k, v):
        head = pl.BlockSpec((None, 1, s, GROUP_W), lambda bi, h: (h, bi, 0, 0))
        return pl.pallas_call(
            functools.partial(_diff_attn_bounded_kernel, lambda_init=lambda_init),
            grid=(b, HEAD_GROUPS),
            in_specs=[_const_spec(lam_p.shape), _const_spec((VAL_DIM, 1)), head, head, head],
            out_specs=head,
            out_shape=out_shape,
            scratch_shapes=[pltpu.VMEM((ATTN_HEADS_PER_STEP * VT_ROWS, s), BF16)],
            compiler_params=pltpu.CompilerParams(
                dimension_semantics=("arbitrary", "arbitrary"), vmem_limit_bytes=VMEM_LIMIT),
            name="diff_attn_bounded",
        )(lam_p, subln_g.reshape(VAL_DIM, 1), q, k, v)

    def online(q, k, v):
        hps = ATTN_HEADS_PER_STEP
        head = pl.BlockSpec((None, 1, s, VAL_DIM), lambda bi, h, i: (h // hps, bi, 0, h % hps))
        blk = pl.BlockSpec((None, 1, ATTN_TQ, VAL_DIM),
                           lambda bi, h, i: (h // hps, bi, i, h % hps))
        return pl.pallas_call(
            functools.partial(_diff_attn_online_kernel, lambda_init=lambda_init),
            grid=(b, DIFF_HEADS, s // ATTN_TQ),
            in_specs=[_const_spec(lam_p.shape), _const_spec((1, VAL_DIM)), blk, head, head],
            out_specs=blk,
            out_shape=out_shape,
            compiler_params=pltpu.CompilerParams(
                dimension_semantics=("arbitrary", "arbitrary", "arbitrary"),
                vmem_limit_bytes=VMEM_LIMIT),
            name="diff_attn_online",
        )(lam_p, subln_g, q, k, v)

    return lax.cond(logit_bound <= FAST_MAX_LOG2, bounded, online, q, k, v)


def _tile_gain(g, reps):
    return jnp.tile(g.reshape(1, -1), (1, reps))


def kernel(x, mem, ffn_norm, ffn_w_in, ffn_w_out, mix_norm, mem_norm, w_mem_kv,
           memq_norm, memk_norm, w_out, a_w_in, a_v_norm, a_w_s, a_b_s,
           b_w_in, b_q_norm, b_k_norm, b_lambda, b_subln):
    batch, seq, d = x.shape
    depth = ffn_norm.shape[0]
    x2d = x.reshape(batch * seq, d)

    n_steps = (batch * seq) // TM_FFN
    ffn_order = [(i, half) for i in range(depth) for half in range(2)]
    ffn_gain = ffn_norm.reshape(depth, 2, 1, d)

    gk_mem = jnp.tile(memk_norm.reshape(depth, 1, HEAD_DIM), (1, 1, MEM_HEADS))
    kmem, vmem, w_in_bf, w_out_bf = _memkv(
        mem, mem_norm.reshape(depth, 1, d), w_mem_kv.astype(BF16), gk_mem,
        lambda steps, flat: _ffn_weight_riders(ffn_w_in, ffn_w_out, ffn_order[0], steps, flat))

    def ffn(x2d, pos, w_in_bf, w_out_bf, extra_riders=()):
        riders = []
        if pos + 1 < len(ffn_order):
            riders = _ffn_weight_riders(ffn_w_in, ffn_w_out, ffn_order[pos + 1], n_steps,
                                        lambda i: i)
        n_next = len(riders)
        res = _ffn(x2d, _layer_const(ffn_gain, ffn_order[pos]), w_in_bf, w_out_bf,
                   riders + list(extra_riders))
        nxt = res[1:1 + n_next] if n_next else [None, None]
        return res[0], nxt[0], nxt[1], res[1 + n_next:]

    for i in range(depth):
        j = i // N_MIXERS
        g_mix = _const(mix_norm[i].reshape(1, d))
        gq_mem = _const(_tile_gain(memq_norm[i], MEM_HEADS))
        tm = TM_GMLP if i % N_MIXERS == 0 else TM_MIX
        k_i, v_i = _mem_spec(kmem, i, seq, tm), _mem_spec(vmem, i, seq, tm)
        w_mix_in = a_w_in if i % N_MIXERS == 0 else b_w_in
        mix_riders = [_cast_rider(w_mix_in, (j,), n_steps, lambda s: s),
                      _cast_rider(w_out, (i,), n_steps, lambda s: s)]
        x2d, w_in_bf, w_out_bf, (w_mix_in_bf, w_o_bf) = ffn(x2d, 2 * i, w_in_bf, w_out_bf,
                                                            mix_riders)
        w_o = _const(w_o_bf)
        if i % N_MIXERS == 0:
            mix = [g_mix, _const(w_mix_in_bf), _const(a_v_norm[j].reshape(1, TOK_W)),
                   _const(a_w_s[j]), _const(a_b_s[j].T), gq_mem]
            x2d = _gmlp_mix(x2d, mix, k_i, v_i, w_o)
        else:
            lambda_init = 0.8 - 0.6 * math.exp(-0.3 * i)
            mix = [g_mix, _const(w_mix_in_bf),
                   _const(_tile_gain(b_q_norm[j], 2 * DIFF_HEADS)),
                   _const(_tile_gain(b_k_norm[j], 2 * DIFF_HEADS)), gq_mem]
            q, k, v, qm = _diff_inproj(x2d, mix)
            logit_bound = (1.01 * HEAD_DIM * ATTN_SCALE * LOG2E
                           * jnp.max(jnp.abs(b_q_norm[j])) * jnp.max(jnp.abs(b_k_norm[j])))
            per_row = (HEAD_GROUPS, batch, seq, GROUP_W)
            tok = _diff_attn(q.reshape(per_row), k.reshape(per_row), v.reshape(per_row),
                             b_lambda[j], b_subln[j].reshape(1, VAL_DIM), lambda_init,
                             logit_bound)
            x2d = _diff_outproj(x2d, tok.reshape(HEAD_GROUPS, batch * seq, GROUP_W), qm,
                                k_i, v_i, w_o)
        x2d, w_in_bf, w_out_bf, _ = ffn(x2d, 2 * i + 1, w_in_bf, w_out_bf)
    return x2d.reshape(batch, seq, d)
```

```python
import functools
import math

import jax
import jax.numpy as jnp
from jax import lax
from jax.experimental import pallas as pl
from jax.experimental.pallas import tpu as pltpu

D_MODEL = 1024
HEAD_DIM = 64
MEM_HEADS = 4
MEM_W = MEM_HEADS * HEAD_DIM
TOK_W = D_MODEL - MEM_W
CHUNK = 128
GROUP_DIM = 128
GROUPS = TOK_W // GROUP_DIM
DIFF_HEADS = TOK_W // (2 * HEAD_DIM)
VAL_DIM = 2 * HEAD_DIM
N_MIXERS = 2
EPS = 1e-6
ATTN_SCALE = HEAD_DIM ** -0.5
LOG2E = math.log2(math.e)
FAST_MAX_LOG2 = 60.0

LANES = 128

SUB = 256
TM_FFN = 1024
TM_GMLP = 1024
TM_MIX = 2048
BF16_ROWS = 16
ATTN_TQ = 256
ATTN_TK = 256
ATTN_BQ = 512
ATTN_HEADS_PER_STEP = 3
HEAD_GROUPS = DIFF_HEADS // ATTN_HEADS_PER_STEP
GROUP_W = ATTN_HEADS_PER_STEP * VAL_DIM
VT_ROWS = VAL_DIM + BF16_ROWS
VMEM_LIMIT = 56 * 1024 * 1024

F32 = jnp.float32
BF16 = jnp.bfloat16


def _const_spec(shape):
    nd = len(shape)
    return pl.BlockSpec(shape, lambda *_: (0,) * nd, pipeline_mode=pl.Buffered(1))


def _rms(x, g):
    ms = jnp.mean(x * x, axis=-1, keepdims=True)
    return x * lax.rsqrt(ms + EPS) * g


def _dot(a, b):
    return jnp.dot(a, b, preferred_element_type=F32)


def _dot_nt(a, b):
    return lax.dot_general(a, b, (((1,), (1,)), ((), ())), preferred_element_type=F32)


def _group_norm64(x, g):
    low = lax.broadcasted_iota(jnp.int32, (1, LANES), 1) < HEAD_DIM
    outs = []
    for j in range(x.shape[-1] // LANES):
        xs = x[:, j * LANES:(j + 1) * LANES]
        sq = xs * xs
        s_lo = jnp.sum(jnp.where(low, sq, 0.0), axis=-1, keepdims=True)
        s_hi = jnp.sum(jnp.where(low, 0.0, sq), axis=-1, keepdims=True)
        r_lo = lax.rsqrt(s_lo * (1.0 / HEAD_DIM) + EPS)
        r_hi = lax.rsqrt(s_hi * (1.0 / HEAD_DIM) + EPS)
        outs.append(xs * jnp.where(low, r_lo, r_hi))
    return jnp.concatenate(outs, axis=-1) * g


def _skewed_trace(stages, tiles):
    states = [{} for _ in tiles]
    for t in range(len(tiles) + len(stages) - 1):
        for j in range(len(tiles)):
            if 0 <= t - j < len(stages):
                stages[t - j](states[j], tiles[j])


def _gelu_exact(x):
    return 0.5 * x * (1.0 + lax.erf(x * math.sqrt(0.5)))


def _read(ref):
    return lambda st, r0: ref[r0:r0 + SUB, :]


def _write(ref):
    def write(st, r0, val):
        ref[r0:r0 + SUB, :] = val
    return write


def _ffn_stages(read_x, write_out, g_ref, win_ref, wout_ref):
    d_ff = wout_ref.shape[0]

    def up_proj(st, r0):
        st["ffn_x"] = read_x(st, r0)
        h = _rms(st["ffn_x"], g_ref[...]).astype(BF16)
        st["gate"] = _dot(h, win_ref[:, :d_ff])
        st["up"] = _dot(h, win_ref[:, d_ff:])

    def activate(st, r0):
        gate = st.pop("gate")
        st["act"] = (gate * jax.nn.sigmoid(gate) * st.pop("up")).astype(BF16)

    def down_proj(st, r0):
        write_out(st, r0, st.pop("ffn_x") + 0.5 * _dot(st.pop("act"), wout_ref[...]))

    return [up_proj, activate, down_proj]


def _mem_attention(qn, k_ref, v_ref):
    k, v = k_ref[0], v_ref[0]
    lane = lax.broadcasted_iota(jnp.int32, (1, MEM_W), 1)
    out = jnp.zeros((qn.shape[0], MEM_W), F32)
    for h in range(MEM_HEADS):
        sel = (lane >= h * HEAD_DIM) & (lane < (h + 1) * HEAD_DIM)
        qh = jnp.where(sel, qn, jnp.zeros_like(qn))
        s = _dot_nt(qh, k)
        p = jnp.exp(s - jnp.max(s, axis=-1, keepdims=True))
        l = jnp.sum(p, axis=-1, keepdims=True)
        vh = jnp.where(sel, v, jnp.zeros_like(v))
        out = out + _dot(p.astype(BF16), vh) / l
    return out


def _gmlp_stages(read_x, write_out, g_ref, win_ref, vg_ref, ws_ref, bst_ref, gq_ref,
                 k_ref, v_ref, wout_ref):
    n_chunks = SUB // CHUNK
    row = lax.broadcasted_iota(jnp.int32, (CHUNK, CHUNK), 0)
    col = lax.broadcasted_iota(jnp.int32, (CHUNK, CHUNK), 1)
    ws = [jnp.where(col <= row, ws_ref[g], 0.0).astype(BF16) for g in range(GROUPS)]
    bst = bst_ref[...]

    def in_proj(st, r0):
        st["mix_x"] = read_x(st, r0)
        h = _rms(st["mix_x"], g_ref[...]).astype(BF16)
        st["z"] = _dot(h, win_ref[...])

    def activate(st, r0):
        z = st.pop("z")
        st["u"] = _gelu_exact(z[:, :TOK_W])
        vact = _gelu_exact(z[:, TOK_W:2 * TOK_W])
        st["vn"] = [_rms(vact[:, g * GROUP_DIM:(g + 1) * GROUP_DIM],
                         vg_ref[:, g * GROUP_DIM:(g + 1) * GROUP_DIM]).astype(BF16)
                    for g in range(GROUPS)]
        st["qn"] = _group_norm64(z[:, 2 * TOK_W:], gq_ref[...]).astype(BF16)

    def spatial(st, r0):
        u, vn = st.pop("u"), st.pop("vn")
        tok_cols = []
        for g in range(GROUPS):
            vg = jnp.concatenate([vn[g][n * CHUNK:(n + 1) * CHUNK] for n in range(n_chunks)],
                                 axis=1)
            mixed = _dot(ws[g], vg) + bst[:, g:g + 1]
            mixed = jnp.concatenate(
                [mixed[:, n * GROUP_DIM:(n + 1) * GROUP_DIM] for n in range(n_chunks)], axis=0)
            tok_cols.append(u[:, g * GROUP_DIM:(g + 1) * GROUP_DIM] * mixed)
        st["tok"] = jnp.concatenate(tok_cols, axis=-1).astype(BF16)

    def mem_attn(st, r0):
        st["mo"] = _mem_attention(st.pop("qn"), k_ref, v_ref).astype(BF16)

    def out_proj(st, r0):
        y = _dot(st.pop("tok"), wout_ref[:TOK_W, :]) + _dot(st.pop("mo"), wout_ref[TOK_W:, :])
        write_out(st, r0, st.pop("mix_x") + y)

    return [in_proj, activate, spatial, mem_attn, out_proj]


def _diff_inproj_stages(read_x, g_ref, win_ref, gq_ref, gk_ref, gqm_ref,
                        q_ref, k_ref, v_ref, qm_ref):
    def in_proj(st, r0):
        h = _rms(read_x(st, r0), g_ref[...]).astype(BF16)
        st["z"] = _dot(h, win_ref[...])

    def normalise(st, r0):
        rs = slice(r0, r0 + SUB)
        z = st.pop("z")
        q = _group_norm64(z[:, :TOK_W], gq_ref[...]) * (ATTN_SCALE * LOG2E)
        k = _group_norm64(z[:, TOK_W:2 * TOK_W], gk_ref[...])
        v = z[:, 2 * TOK_W:3 * TOK_W]
        for g in range(HEAD_GROUPS):
            cs = slice(g * GROUP_W, (g + 1) * GROUP_W)
            q_ref[g, rs, :] = q[:, cs].astype(BF16)
            k_ref[g, rs, :] = k[:, cs].astype(BF16)
            v_ref[g, rs, :] = v[:, cs].astype(BF16)
        qm_ref[rs, :] = _group_norm64(z[:, 3 * TOK_W:], gqm_ref[...]).astype(BF16)

    return [in_proj, normalise]


def _diff_outproj_stages(read_x, write_out, tok_ref, qm_ref, k_ref, v_ref, wout_ref):
    def mem_attn(st, r0):
        st["mo"] = _mem_attention(qm_ref[r0:r0 + SUB, :], k_ref, v_ref).astype(BF16)

    def out_proj(st, r0):
        y = _dot(st.pop("mo"), wout_ref[TOK_W:, :])
        for g in range(HEAD_GROUPS):
            y = y + _dot(tok_ref[g, r0:r0 + SUB, :], wout_ref[g * GROUP_W:(g + 1) * GROUP_W, :])
        write_out(st, r0, read_x(st, r0) + y)

    return [mem_attn, out_proj]


def _run_stages(stages, rows):
    _skewed_trace(stages, list(range(0, rows, SUB)))


def _ffn_kernel(x_ref, fg_ref, fwin_ref, fwout_ref, o_ref):
    _run_stages(_ffn_stages(_read(x_ref), _write(o_ref), fg_ref, fwin_ref, fwout_ref),
                x_ref.shape[0])


def _gmlp_kernel(x_ref, g_ref, win_ref, vg_ref, ws_ref, bst_ref, gq_ref, k_ref, v_ref,
                 wout_ref, o_ref):
    _run_stages(_gmlp_stages(_read(x_ref), _write(o_ref), g_ref, win_ref, vg_ref, ws_ref,
                             bst_ref, gq_ref, k_ref, v_ref, wout_ref), x_ref.shape[0])


def _diff_inproj_kernel(x_ref, g_ref, win_ref, gq_ref, gk_ref, gqm_ref,
                        q_ref, k_ref, v_ref, qm_ref):
    _run_stages(_diff_inproj_stages(_read(x_ref), g_ref, win_ref, gq_ref, gk_ref, gqm_ref,
                                    q_ref, k_ref, v_ref, qm_ref), x_ref.shape[0])


def _diff_outproj_kernel(x_ref, tok_ref, qm_ref, k_ref, v_ref, wout_ref, o_ref):
    _run_stages(_diff_outproj_stages(_read(x_ref), _write(o_ref), tok_ref, qm_ref, k_ref,
                                     v_ref, wout_ref), x_ref.shape[0])


def _cast_blocks(rows, n_steps):
    for blocks in range(n_steps, 0, -1):
        if n_steps % blocks == 0 and rows % (blocks * BF16_ROWS) == 0:
            return blocks
    raise ValueError(f"{rows} rows cannot be split into bf16 row blocks over {n_steps} steps")


def _cast_rider(stacked, lead, n_steps, flat_step):
    rows, cols = stacked.shape[len(lead):]
    blocks = _cast_blocks(rows, n_steps)
    steps_per_block = n_steps // blocks
    squeezed = (None,) * len(lead)

    def src_index(*g):
        return lead + (flat_step(*g) // steps_per_block, 0)

    def dst_index(*g):
        return (flat_step(*g) // steps_per_block, 0)

    src = pl.BlockSpec(squeezed + (rows // blocks, cols), src_index)
    dst = pl.BlockSpec((rows // blocks, cols), dst_index)
    return (stacked, src), (jax.ShapeDtypeStruct((rows, cols), BF16), dst)


def _with_riders(kernel_fn, n_in, n_out, n_riders):
    def kern(*refs):
        ins, rin = refs[:n_in], refs[n_in:n_in + n_riders]
        outs = refs[n_in + n_riders:n_in + n_riders + n_out]
        rout = refs[n_in + n_riders + n_out:n_in + 2 * n_riders + n_out]
        for src, dst in zip(rin, rout):
            dst[...] = src[...].astype(BF16)
        kernel_fn(*ins, *outs, *refs[n_in + 2 * n_riders + n_out:])
    return kern


def _ffn_weight_riders(ffn_w_in, ffn_w_out, lead, n_steps, flat_step):
    return [_cast_rider(ffn_w_in, lead, n_steps, flat_step),
            _cast_rider(ffn_w_out, lead, n_steps, flat_step)]


def _row_spec(tm, width):
    return pl.BlockSpec((tm, width), lambda i: (i, 0))


def _group_row_spec(tm):
    return pl.BlockSpec((HEAD_GROUPS, tm, GROUP_W), lambda i: (0, i, 0))


def _layer_const(stacked, lead):
    tail = stacked.shape[len(lead):]
    spec = pl.BlockSpec((None,) * len(lead) + tail, lambda *_: lead + (0,) * len(tail),
                        pipeline_mode=pl.Buffered(1))
    return (stacked, spec)


def _mem_spec(stacked, layer, seq, tm):
    tiles_per_row = seq // tm
    _, _, l, w = stacked.shape
    return (stacked, pl.BlockSpec((None, 1, l, w), lambda i: (layer, i // tiles_per_row, 0, 0)))


def _const(arr):
    return (arr, _const_spec(arr.shape))


def _token_call(kernel_fn, name, n, tm, ins, outs, riders=()):
    rin = [r[0] for r in riders]
    rout = [r[1] for r in riders]
    if riders:
        kernel_fn = _with_riders(kernel_fn, len(ins), len(outs), len(riders))
    return pl.pallas_call(
        kernel_fn,
        grid=(n // tm,),
        in_specs=[spec for _, spec in ins + rin],
        out_specs=[spec for _, spec in outs + rout],
        out_shape=[shape for shape, _ in outs + rout],
        compiler_params=pltpu.CompilerParams(
            dimension_semantics=("arbitrary",), vmem_limit_bytes=VMEM_LIMIT),
        name=name,
    )(*[arr for arr, _ in ins + rin])


def _ffn(x2d, gain, w_in, w_out, riders):
    n, d = x2d.shape
    return _token_call(
        _ffn_kernel, "ffn", n, TM_FFN,
        [(x2d, _row_spec(TM_FFN, d)), gain, _const(w_in), _const(w_out)],
        [(jax.ShapeDtypeStruct((n, d), F32), _row_spec(TM_FFN, d))], riders)


def _gmlp_mix(x2d, mix, kmem, vmem, w_o):
    n, d = x2d.shape
    (out,) = _token_call(
        _gmlp_kernel, "gmlp_mix", n, TM_GMLP,
        [(x2d, _row_spec(TM_GMLP, d))] + mix + [kmem, vmem, w_o],
        [(jax.ShapeDtypeStruct((n, d), F32), _row_spec(TM_GMLP, d))])
    return out


def _diff_inproj(x2d, mix):
    n, d = x2d.shape
    grouped = (jax.ShapeDtypeStruct((HEAD_GROUPS, n, GROUP_W), BF16), _group_row_spec(TM_MIX))
    return _token_call(
        _diff_inproj_kernel, "diff_inproj", n, TM_MIX, [(x2d, _row_spec(TM_MIX, d))] + mix,
        [grouped, grouped, grouped,
         (jax.ShapeDtypeStruct((n, MEM_W), BF16), _row_spec(TM_MIX, MEM_W))])


def _diff_outproj(x2d, tok, qm, kmem, vmem, w_o):
    n, d = x2d.shape
    (out,) = _token_call(
        _diff_outproj_kernel, "diff_outproj", n, TM_MIX,
        [(x2d, _row_spec(TM_MIX, d)), (tok, _group_row_spec(TM_MIX)),
         (qm, _row_spec(TM_MIX, MEM_W)), kmem, vmem, w_o],
        [(jax.ShapeDtypeStruct((n, d), F32), _row_spec(TM_MIX, d))])
    return out


def _memkv_kernel(mem_ref, g_ref, w_ref, gk_ref, k_ref, v_ref):
    mh = _rms(mem_ref[0], g_ref[0]).astype(BF16)
    kv = _dot(mh, w_ref[0])
    k = _group_norm64(kv[:, :MEM_W], gk_ref[0]) * ATTN_SCALE
    k_ref[0, 0] = k.astype(BF16)
    v_ref[0, 0] = kv[:, MEM_W:].astype(BF16)


def _memkv(mem, mem_norm, w_kv, gk_tiled, make_riders):
    depth = w_kv.shape[0]
    b, l, d = mem.shape
    kv_out = (jax.ShapeDtypeStruct((depth, b, l, MEM_W), BF16),
              pl.BlockSpec((1, 1, l, MEM_W), lambda i, j: (i, j, 0, 0)))
    riders = make_riders(depth * b, lambda i, j: i * b + j)
    ins = [(mem, pl.BlockSpec((1, l, d), lambda i, j: (j, 0, 0))),
           (mem_norm, pl.BlockSpec((1, 1, d), lambda i, j: (i, 0, 0))),
           (w_kv, pl.BlockSpec((1, d, 2 * MEM_W), lambda i, j: (i, 0, 0))),
           (gk_tiled, pl.BlockSpec((1, 1, MEM_W), lambda i, j: (i, 0, 0)))]
    ins += [r[0] for r in riders]
    outs = [kv_out, kv_out] + [r[1] for r in riders]
    return pl.pallas_call(
        _with_riders(_memkv_kernel, 4, 2, len(riders)),
        grid=(depth, b),
        in_specs=[spec for _, spec in ins],
        out_specs=[spec for _, spec in outs],
        out_shape=[shape for shape, _ in outs],
        compiler_params=pltpu.CompilerParams(
            dimension_semantics=("arbitrary", "arbitrary"), vmem_limit_bytes=VMEM_LIMIT),
        name="memkv",
    )(*[arr for arr, _ in ins])


def _split_components(q):
    lane = lax.broadcasted_iota(jnp.int32, (1, VAL_DIM), 1)
    zero = jnp.zeros_like(q)
    return jnp.concatenate([jnp.where(lane < HEAD_DIM, q, zero),
                            jnp.where(lane >= HEAD_DIM, q, zero)], axis=0)


def _causal_keep(rows, cols):
    r = lax.broadcasted_iota(jnp.int32, (rows, cols), 0)
    c = lax.broadcasted_iota(jnp.int32, (rows, cols), 1)
    return jnp.concatenate([c <= r, c <= r], axis=0)


def _diff_combine(o2, lam_ref, sg_ref, lambda_init):
    rows = o2.shape[0] // 2
    lp = lam_ref[...]
    lam = (jnp.exp(jnp.sum(lp[0:1] * lp[1:2], axis=-1, keepdims=True))
           - jnp.exp(jnp.sum(lp[2:3] * lp[3:4], axis=-1, keepdims=True)) + lambda_init)
    o = o2[:rows] - lam * o2[rows:]
    return (_rms(o, sg_ref[...]) * (1.0 - lambda_init)).astype(BF16)


def _diff_attn_online_kernel(lam_ref, sg_ref, q_ref, k_ref, v_ref, o_ref, *, lambda_init):
    qi = pl.program_id(2)
    qq = _split_components(q_ref[0])

    def scores(j):
        rows = pl.ds(pl.multiple_of(j * ATTN_TK, ATTN_TK), ATTN_TK)
        return _dot_nt(qq, k_ref[0, rows, :]), v_ref[0, rows, :]

    s, vb = scores(qi)
    s = jnp.where(_causal_keep(ATTN_TQ, ATTN_TK), s, -jnp.inf)
    m = jnp.max(s, axis=-1, keepdims=True)
    p = jnp.exp2(s - m)
    l = jnp.sum(p, axis=-1, keepdims=True)
    acc = _dot(p.astype(BF16), vb)

    def body(j, carry):
        m, l, acc = carry
        s, vb = scores(j)
        m_new = jnp.maximum(m, jnp.max(s, axis=-1, keepdims=True))
        alpha = jnp.exp2(m - m_new)
        p = jnp.exp2(s - m_new)
        l = alpha * l + jnp.sum(p, axis=-1, keepdims=True)
        acc = alpha * acc + _dot(p.astype(BF16), vb)
        return m_new, l, acc

    m, l, acc = lax.fori_loop(0, qi, body, (m, l, acc))
    o_ref[0] = _diff_combine(acc / l, lam_ref, sg_ref, lambda_init)


def _diff_attn_bounded_kernel(lam_ref, sg_ref, q_ref, k_ref, v_ref, o_ref, vext_ref,
                              *, lambda_init):
    s_len = q_ref.shape[1]
    row = lax.broadcasted_iota(jnp.int32, (VT_ROWS - VAL_DIM, s_len), 0)
    ones_row = jnp.where(row == 0, 1.0, 0.0).astype(BF16)
    bq = ATTN_BQ
    key_i = lax.broadcasted_iota(jnp.int32, (bq, bq), 0)
    qry_i = lax.broadcasted_iota(jnp.int32, (bq, bq), 1)
    keep = jnp.concatenate([key_i <= qry_i, key_i <= qry_i], axis=1)
    lp = lam_ref[...]
    lam = (jnp.exp(jnp.sum(lp[0:1] * lp[1:2], axis=-1, keepdims=True))
           - jnp.exp(jnp.sum(lp[2:3] * lp[3:4], axis=-1, keepdims=True)) + lambda_init)
    gain_t = sg_ref[...] * (1.0 - lambda_init)

    for hh in range(q_ref.shape[2] // VAL_DIM):
        hl = slice(hh * VAL_DIM, (hh + 1) * VAL_DIM)
        r0 = hh * VT_ROWS
        vext_ref[r0:r0 + VAL_DIM, :] = v_ref[0, :, hl].T
        vext_ref[r0 + VAL_DIM:r0 + VT_ROWS, :] = ones_row
        for i in range(s_len // bq):
            lo, hi = i * bq, (i + 1) * bq
            qq = _split_components(q_ref[0, lo:hi, hl])
            t = _dot_nt(k_ref[0, :hi, hl], qq)
            p = jnp.exp2(jnp.where(keep, t[lo:], -jnp.inf)).astype(BF16)
            if i > 0:
                p = jnp.concatenate([jnp.exp2(t[:lo]).astype(BF16), p], axis=0)
            acc = _dot(vext_ref[r0:r0 + VT_ROWS, :hi], p)
            o2 = acc[:VAL_DIM] * (1.0 / acc[VAL_DIM:VAL_DIM + 1])
            o = o2[:, :bq] - lam * o2[:, bq:]
            ms = jnp.mean(o * o, axis=0, keepdims=True)
            y = o * lax.rsqrt(ms + EPS) * gain_t
            o_ref[0, lo:hi, hl] = y.T.astype(BF16)


def _diff_attn(q, k, v, lam_p, subln_g, lambda_init, logit_bound):
    _, b, s, _ = q.shape
    out_shape = jax.ShapeDtypeStruct(q.shape, BF16)

    def bounded(q, k, v):
        head = pl.BlockSpec((None, 1, s, GROUP_W), lambda bi, h: (h, bi, 0, 0))
        return pl.pallas_call(
            functools.partial(_diff_attn_bounded_kernel, lambda_init=lambda_init),
            grid=(b, HEAD_GROUPS),
            in_specs=[_const_spec(lam_p.shape), _const_spec((VAL_DIM, 1)), head, head, head],
            out_specs=head,
            out_shape=out_shape,
            scratch_shapes=[pltpu.VMEM((ATTN_HEADS_PER_STEP * VT_ROWS, s), BF16)],
            compiler_params=pltpu.CompilerParams(
                dimension_semantics=("arbitrary", "arbitrary"), vmem_limit_bytes=VMEM_LIMIT),
            name="diff_attn_bounded",
        )(lam_p, subln_g.reshape(VAL_DIM, 1), q, k, v)

    def online(q, k, v):
        hps = ATTN_HEADS_PER_STEP
        head = pl.BlockSpec((None, 1, s, VAL_DIM), lambda bi, h, i: (h // hps, bi, 0, h % hps))
        blk = pl.BlockSpec((None, 1, ATTN_TQ, VAL_DIM),
                           lambda bi, h, i: (h // hps, bi, i, h % hps))
        return pl.pallas_call(
            functools.partial(_diff_attn_online_kernel, lambda_init=lambda_init),
            grid=(b, DIFF_HEADS, s // ATTN_TQ),
            in_specs=[_const_spec(lam_p.shape), _const_spec((1, VAL_DIM)), blk, head, head],
            out_specs=blk,
            out_shape=out_shape,
            compiler_params=pltpu.CompilerParams(
                dimension_semantics=("arbitrary", "arbitrary", "arbitrary"),
                vmem_limit_bytes=VMEM_LIMIT),
            name="diff_attn_online",
        )(lam_p, subln_g, q, k, v)

    return lax.cond(logit_bound <= FAST_MAX_LOG2, bounded, online, q, k, v)


def _tile_gain(g, reps):
    return jnp.tile(g.reshape(1, -1), (1, reps))


def kernel(x, mem, ffn_norm, ffn_w_in, ffn_w_out, mix_norm, mem_norm, w_mem_kv,
           memq_norm, memk_norm, w_out, a_w_in, a_v_norm, a_w_s, a_b_s,
           b_w_in, b_q_norm, b_k_norm, b_lambda, b_subln):
    batch, seq, d = x.shape
    depth = ffn_norm.shape[0]
    x2d = x.reshape(batch * seq, d)

    n_steps = (batch * seq) // TM_FFN
    ffn_order = [(i, half) for i in range(depth) for half in range(2)]
    ffn_gain = ffn_norm.reshape(depth, 2, 1, d)

    gk_mem = jnp.tile(memk_norm.reshape(depth, 1, HEAD_DIM), (1, 1, MEM_HEADS))
    kmem, vmem, w_in_bf, w_out_bf = _memkv(
        mem, mem_norm.reshape(depth, 1, d), w_mem_kv.astype(BF16), gk_mem,
        lambda steps, flat: _ffn_weight_riders(ffn_w_in, ffn_w_out, ffn_order[0], steps, flat))

    def ffn(x2d, pos, w_in_bf, w_out_bf, extra_riders=()):
        riders = []
        if pos + 1 < len(ffn_order):
            riders = _ffn_weight_riders(ffn_w_in, ffn_w_out, ffn_order[pos + 1], n_steps,
                                        lambda i: i)
        n_next = len(riders)
        res = _ffn(x2d, _layer_const(ffn_gain, ffn_order[pos]), w_in_bf, w_out_bf,
                   riders + list(extra_riders))
        nxt = res[1:1 + n_next] if n_next else [None, None]
        return res[0], nxt[0], nxt[1], res[1 + n_next:]

    for i in range(depth):
        j = i // N_MIXERS
        g_mix = _const(mix_norm[i].reshape(1, d))
        gq_mem = _const(_tile_gain(memq_norm[i], MEM_HEADS))
        tm = TM_GMLP if i % N_MIXERS == 0 else TM_MIX
        k_i, v_i = _mem_spec(kmem, i, seq, tm), _mem_spec(vmem, i, seq, tm)
        w_mix_in = a_w_in if i % N_MIXERS == 0 else b_w_in
        mix_riders = [_cast_rider(w_mix_in, (j,), n_steps, lambda s: s),
                      _cast_rider(w_out, (i,), n_steps, lambda s: s)]
        x2d, w_in_bf, w_out_bf, (w_mix_in_bf, w_o_bf) = ffn(x2d, 2 * i, w_in_bf, w_out_bf,
                                                            mix_riders)
        w_o = _const(w_o_bf)
        if i % N_MIXERS == 0:
            mix = [g_mix, _const(w_mix_in_bf), _const(a_v_norm[j].reshape(1, TOK_W)),
                   _const(a_w_s[j]), _const(a_b_s[j].T), gq_mem]
            x2d = _gmlp_mix(x2d, mix, k_i, v_i, w_o)
        else:
            lambda_init = 0.8 - 0.6 * math.exp(-0.3 * i)
            mix = [g_mix, _const(w_mix_in_bf),
                   _const(_tile_gain(b_q_norm[j], 2 * DIFF_HEADS)),
                   _const(_tile_gain(b_k_norm[j], 2 * DIFF_HEADS)), gq_mem]
            q, k, v, qm = _diff_inproj(x2d, mix)
            logit_bound = (1.01 * HEAD_DIM * ATTN_SCALE * LOG2E
                           * jnp.max(jnp.abs(b_q_norm[j])) * jnp.max(jnp.abs(b_k_norm[j])))
            per_row = (HEAD_GROUPS, batch, seq, GROUP_W)
            tok = _diff_attn(q.reshape(per_row), k.reshape(per_row), v.reshape(per_row),
                             b_lambda[j], b_subln[j].reshape(1, VAL_DIM), lambda_init,
                             logit_bound)
            x2d = _diff_outproj(x2d, tok.reshape(HEAD_GROUPS, batch * seq, GROUP_W), qm,
                                k_i, v_i, w_o)
        x2d, w_in_bf, w_out_bf, _ = ffn(x2d, 2 * i + 1, w_in_bf, w_out_bf)
    return x2d.reshape(batch, seq, d)
```

```python
import functools
import math

import jax
import jax.numpy as jnp
from jax import lax
from jax.experimental import pallas as pl
from jax.experimental.pallas import tpu as pltpu

D_MODEL = 1024
HEAD_DIM = 64
MEM_HEADS = 4
MEM_W = MEM_HEADS * HEAD_DIM
TOK_W = D_MODEL - MEM_W
CHUNK = 128
GROUP_DIM = 128
GROUPS = TOK_W // GROUP_DIM
DIFF_HEADS = TOK_W // (2 * HEAD_DIM)
VAL_DIM = 2 * HEAD_DIM
N_MIXERS = 2
EPS = 1e-6
ATTN_SCALE = HEAD_DIM ** -0.5
LOG2E = math.log2(math.e)
FAST_MAX_LOG2 = 60.0

LANES = 128

SUB_FFN = 256
SUB_GMLP = 256
SUB_MIX = 512
TM_FFN = 1024
TM_GMLP = 1024
TM_MIX = 2048
BF16_ROWS = 16
ATTN_TQ = 256
ATTN_TK = 256
ATTN_BQ = 512
ATTN_HEADS_PER_STEP = 3
HEAD_GROUPS = DIFF_HEADS // ATTN_HEADS_PER_STEP
GROUP_W = ATTN_HEADS_PER_STEP * VAL_DIM
VT_ROWS = VAL_DIM + BF16_ROWS
VMEM_LIMIT = 56 * 1024 * 1024

F32 = jnp.float32
BF16 = jnp.bfloat16


def _const_spec(shape):
    nd = len(shape)
    return pl.BlockSpec(shape, lambda *_: (0,) * nd, pipeline_mode=pl.Buffered(1))


def _rms(x, g):
    ms = jnp.mean(x * x, axis=-1, keepdims=True)
    return x * lax.rsqrt(ms + EPS) * g


def _dot(a, b):
    return jnp.dot(a, b, preferred_element_type=F32)


def _dot_nt(a, b):
    return lax.dot_general(a, b, (((1,), (1,)), ((), ())), preferred_element_type=F32)


def _group_norm64(x, g):
    low = lax.broadcasted_iota(jnp.int32, (1, LANES), 1) < HEAD_DIM
    outs = []
    for j in range(x.shape[-1] // LANES):
        xs = x[:, j * LANES:(j + 1) * LANES]
        sq = xs * xs
        s_lo = jnp.sum(jnp.where(low, sq, 0.0), axis=-1, keepdims=True)
        s_hi = jnp.sum(jnp.where(low, 0.0, sq), axis=-1, keepdims=True)
        r_lo = lax.rsqrt(s_lo * (1.0 / HEAD_DIM) + EPS)
        r_hi = lax.rsqrt(s_hi * (1.0 / HEAD_DIM) + EPS)
        outs.append(xs * jnp.where(low, r_lo, r_hi))
    return jnp.concatenate(outs, axis=-1) * g


def _skewed_trace(stages, tiles):
    states = [{} for _ in tiles]
    for t in range(len(tiles) + len(stages) - 1):
        for j in range(len(tiles)):
            if 0 <= t - j < len(stages):
                stages[t - j](states[j], tiles[j])


def _gelu_exact(x):
    return 0.5 * x * (1.0 + lax.erf(x * math.sqrt(0.5)))


def _read(ref, sub):
    return lambda st, r0: ref[r0:r0 + sub, :]


def _write(ref, sub):
    def write(st, r0, val):
        ref[r0:r0 + sub, :] = val
    return write


def _ffn_stages(read_x, write_out, g_ref, win_ref, wout_ref):
    d_ff = wout_ref.shape[0]

    def up_proj(st, r0):
        st["ffn_x"] = read_x(st, r0)
        h = _rms(st["ffn_x"], g_ref[...]).astype(BF16)
        st["gate_up"] = _dot(h, win_ref[...])

    def activate(st, r0):
        gate_up = st.pop("gate_up")
        gate, up = gate_up[:, :d_ff], gate_up[:, d_ff:]
        st["act"] = (gate * jax.nn.sigmoid(gate) * up).astype(BF16)

    def down_proj(st, r0):
        write_out(st, r0, st.pop("ffn_x") + 0.5 * _dot(st.pop("act"), wout_ref[...]))

    return [up_proj, activate, down_proj]


def _mem_attention(qn, k_ref, v_ref):
    rows = qn.shape[0]
    lane = lax.broadcasted_iota(jnp.int32, (1, MEM_W), 1)
    sels = [(lane >= h * HEAD_DIM) & (lane < (h + 1) * HEAD_DIM) for h in range(MEM_HEADS)]
    qs = jnp.concatenate([jnp.where(sel, qn, jnp.zeros_like(qn)) for sel in sels], axis=0)
    s = _dot_nt(qs, k_ref[0])
    p = jnp.exp(s - jnp.max(s, axis=-1, keepdims=True))
    l = jnp.sum(p, axis=-1, keepdims=True)
    o = _dot(p.astype(BF16), v_ref[0]) / l
    out = jnp.zeros((rows, MEM_W), F32)
    for h, sel in enumerate(sels):
        out = out + jnp.where(sel, o[h * rows:(h + 1) * rows], 0.0)
    return out


def _gmlp_stages(read_x, write_out, g_ref, win_ref, vg_ref, ws_ref, bst_ref, gq_ref,
                 k_ref, v_ref, wout_ref, sub):
    n_chunks = sub // CHUNK
    row = lax.broadcasted_iota(jnp.int32, (CHUNK, CHUNK), 0)
    col = lax.broadcasted_iota(jnp.int32, (CHUNK, CHUNK), 1)
    ws = [jnp.where(col <= row, ws_ref[g], 0.0).astype(BF16) for g in range(GROUPS)]
    bst = bst_ref[...]

    def in_proj(st, r0):
        st["mix_x"] = read_x(st, r0)
        h = _rms(st["mix_x"], g_ref[...]).astype(BF16)
        st["z"] = _dot(h, win_ref[...])

    def activate(st, r0):
        z = st.pop("z")
        st["u"] = _gelu_exact(z[:, :TOK_W])
        vact = _gelu_exact(z[:, TOK_W:2 * TOK_W])
        st["vn"] = [_rms(vact[:, g * GROUP_DIM:(g + 1) * GROUP_DIM],
                         vg_ref[:, g * GROUP_DIM:(g + 1) * GROUP_DIM]).astype(BF16)
                    for g in range(GROUPS)]
        st["qn"] = _group_norm64(z[:, 2 * TOK_W:], gq_ref[...]).astype(BF16)

    def spatial(st, r0):
        u, vn = st.pop("u"), st.pop("vn")
        tok_cols = []
        for g in range(GROUPS):
            vg = jnp.concatenate([vn[g][n * CHUNK:(n + 1) * CHUNK] for n in range(n_chunks)],
                                 axis=1)
            mixed = _dot(ws[g], vg) + bst[:, g:g + 1]
            mixed = jnp.concatenate(
                [mixed[:, n * GROUP_DIM:(n + 1) * GROUP_DIM] for n in range(n_chunks)], axis=0)
            tok_cols.append(u[:, g * GROUP_DIM:(g + 1) * GROUP_DIM] * mixed)
        st["tok"] = jnp.concatenate(tok_cols, axis=-1).astype(BF16)

    def mem_attn(st, r0):
        st["mo"] = _mem_attention(st.pop("qn"), k_ref, v_ref).astype(BF16)

    def out_proj(st, r0):
        y = _dot(jnp.concatenate([st.pop("tok"), st.pop("mo")], axis=1), wout_ref[...])
        write_out(st, r0, st.pop("mix_x") + y)

    return [in_proj, activate, spatial, mem_attn, out_proj]


def _diff_inproj_stages(read_x, g_ref, win_ref, gq_ref, gk_ref, gqm_ref,
                        q_ref, k_ref, v_ref, qm_ref, sub):
    def in_proj(st, r0):
        h = _rms(read_x(st, r0), g_ref[...]).astype(BF16)
        st["z"] = _dot(h, win_ref[...])

    def normalise(st, r0):
        rs = slice(r0, r0 + sub)
        z = st.pop("z")
        q = _group_norm64(z[:, :TOK_W], gq_ref[...]) * (ATTN_SCALE * LOG2E)
        k = _group_norm64(z[:, TOK_W:2 * TOK_W], gk_ref[...])
        v = z[:, 2 * TOK_W:3 * TOK_W]
        for g in range(HEAD_GROUPS):
            cs = slice(g * GROUP_W, (g + 1) * GROUP_W)
            q_ref[g, rs, :] = q[:, cs].astype(BF16)
            k_ref[g, rs, :] = k[:, cs].astype(BF16)
            v_ref[g, rs, :] = v[:, cs].astype(BF16)
        qm_ref[rs, :] = _group_norm64(z[:, 3 * TOK_W:], gqm_ref[...]).astype(BF16)

    return [in_proj, normalise]


def _diff_outproj_stages(read_x, write_out, tok_ref, qm_ref, k_ref, v_ref, wout_ref, sub):
    def mem_attn(st, r0):
        st["mo"] = _mem_attention(qm_ref[r0:r0 + sub, :], k_ref, v_ref).astype(BF16)

    def out_proj(st, r0):
        cat = [tok_ref[g, r0:r0 + sub, :] for g in range(HEAD_GROUPS)] + [st.pop("mo")]
        y = _dot(jnp.concatenate(cat, axis=1), wout_ref[...])
        write_out(st, r0, read_x(st, r0) + y)

    return [mem_attn, out_proj]


def _run_stages(stages, rows, sub):
    _skewed_trace(stages, list(range(0, rows, sub)))


def _ffn_kernel(x_ref, fg_ref, fwin_ref, fwout_ref, o_ref):
    sub = SUB_FFN
    _run_stages(_ffn_stages(_read(x_ref, sub), _write(o_ref, sub), fg_ref, fwin_ref, fwout_ref),
                x_ref.shape[0], sub)


def _gmlp_kernel(x_ref, g_ref, win_ref, vg_ref, ws_ref, bst_ref, gq_ref, k_ref, v_ref,
                 wout_ref, o_ref):
    sub = SUB_GMLP
    _run_stages(_gmlp_stages(_read(x_ref, sub), _write(o_ref, sub), g_ref, win_ref, vg_ref,
                             ws_ref, bst_ref, gq_ref, k_ref, v_ref, wout_ref, sub),
                x_ref.shape[0], sub)


def _diff_inproj_kernel(x_ref, g_ref, win_ref, gq_ref, gk_ref, gqm_ref,
                        q_ref, k_ref, v_ref, qm_ref):
    sub = SUB_MIX
    _run_stages(_diff_inproj_stages(_read(x_ref, sub), g_ref, win_ref, gq_ref, gk_ref, gqm_ref,
                                    q_ref, k_ref, v_ref, qm_ref, sub), x_ref.shape[0], sub)


def _diff_outproj_kernel(x_ref, tok_ref, qm_ref, k_ref, v_ref, wout_ref, o_ref):
    sub = SUB_MIX
    _run_stages(_diff_outproj_stages(_read(x_ref, sub), _write(o_ref, sub), tok_ref, qm_ref,
                                     k_ref, v_ref, wout_ref, sub), x_ref.shape[0], sub)


def _cast_blocks(rows, n_steps):
    for blocks in range(n_steps, 0, -1):
        if n_steps % blocks == 0 and rows % (blocks * BF16_ROWS) == 0:
            return blocks
    raise ValueError(f"{rows} rows cannot be split into bf16 row blocks over {n_steps} steps")


def _cast_rider(stacked, lead, n_steps, flat_step):
    rows, cols = stacked.shape[len(lead):]
    blocks = _cast_blocks(rows, n_steps)
    steps_per_block = n_steps // blocks
    squeezed = (None,) * len(lead)

    def src_index(*g):
        return lead + (flat_step(*g) // steps_per_block, 0)

    def dst_index(*g):
        return (flat_step(*g) // steps_per_block, 0)

    src = pl.BlockSpec(squeezed + (rows // blocks, cols), src_index)
    dst = pl.BlockSpec((rows // blocks, cols), dst_index)
    return (stacked, src), (jax.ShapeDtypeStruct((rows, cols), BF16), dst)


def _with_riders(kernel_fn, n_in, n_out, n_riders):
    def kern(*refs):
        ins, rin = refs[:n_in], refs[n_in:n_in + n_riders]
        outs = refs[n_in + n_riders:n_in + n_riders + n_out]
        rout = refs[n_in + n_riders + n_out:n_in + 2 * n_riders + n_out]
        for src, dst in zip(rin, rout):
            dst[...] = src[...].astype(BF16)
        kernel_fn(*ins, *outs, *refs[n_in + 2 * n_riders + n_out:])
    return kern


def _ffn_weight_riders(ffn_w_in, ffn_w_out, lead, n_steps, flat_step):
    return [_cast_rider(ffn_w_in, lead, n_steps, flat_step),
            _cast_rider(ffn_w_out, lead, n_steps, flat_step)]


def _row_spec(tm, width):
    return pl.BlockSpec((tm, width), lambda i: (i, 0))


def _group_row_spec(tm):
    return pl.BlockSpec((HEAD_GROUPS, tm, GROUP_W), lambda i: (0, i, 0))


def _layer_const(stacked, lead):
    tail = stacked.shape[len(lead):]
    spec = pl.BlockSpec((None,) * len(lead) + tail, lambda *_: lead + (0,) * len(tail),
                        pipeline_mode=pl.Buffered(1))
    return (stacked, spec)


def _mem_spec(stacked, layer, seq, tm):
    tiles_per_row = seq // tm
    _, _, l, w = stacked.shape
    return (stacked, pl.BlockSpec((None, 1, l, w), lambda i: (layer, i // tiles_per_row, 0, 0)))


def _const(arr):
    return (arr, _const_spec(arr.shape))


def _token_call(kernel_fn, name, n, tm, ins, outs, riders=()):
    rin = [r[0] for r in riders]
    rout = [r[1] for r in riders]
    if riders:
        kernel_fn = _with_riders(kernel_fn, len(ins), len(outs), len(riders))
    return pl.pallas_call(
        kernel_fn,
        grid=(n // tm,),
        in_specs=[spec for _, spec in ins + rin],
        out_specs=[spec for _, spec in outs + rout],
        out_shape=[shape for shape, _ in outs + rout],
        compiler_params=pltpu.CompilerParams(
            dimension_semantics=("arbitrary",), vmem_limit_bytes=VMEM_LIMIT),
        name=name,
    )(*[arr for arr, _ in ins + rin])


def _ffn(x2d, gain, w_in, w_out, riders):
    n, d = x2d.shape
    return _token_call(
        _ffn_kernel, "ffn", n, TM_FFN,
        [(x2d, _row_spec(TM_FFN, d)), gain, _const(w_in), _const(w_out)],
        [(jax.ShapeDtypeStruct((n, d), F32), _row_spec(TM_FFN, d))], riders)


def _gmlp_mix(x2d, mix, kmem, vmem, w_o):
    n, d = x2d.shape
    (out,) = _token_call(
        _gmlp_kernel, "gmlp_mix", n, TM_GMLP,
        [(x2d, _row_spec(TM_GMLP, d))] + mix + [kmem, vmem, w_o],
        [(jax.ShapeDtypeStruct((n, d), F32), _row_spec(TM_GMLP, d))])
    return out


def _diff_inproj(x2d, mix):
    n, d = x2d.shape
    grouped = (jax.ShapeDtypeStruct((HEAD_GROUPS, n, GROUP_W), BF16), _group_row_spec(TM_MIX))
    return _token_call(
        _diff_inproj_kernel, "diff_inproj", n, TM_MIX, [(x2d, _row_spec(TM_MIX, d))] + mix,
        [grouped, grouped, grouped,
         (jax.ShapeDtypeStruct((n, MEM_W), BF16), _row_spec(TM_MIX, MEM_W))])


def _diff_outproj(x2d, tok, qm, kmem, vmem, w_o):
    n, d = x2d.shape
    (out,) = _token_call(
        _diff_outproj_kernel, "diff_outproj", n, TM_MIX,
        [(x2d, _row_spec(TM_MIX, d)), (tok, _group_row_spec(TM_MIX)),
         (qm, _row_spec(TM_MIX, MEM_W)), kmem, vmem, w_o],
        [(jax.ShapeDtypeStruct((n, d), F32), _row_spec(TM_MIX, d))])
    return out


def _memkv_kernel(mem_ref, g_ref, w_ref, gk_ref, k_ref, v_ref):
    mh = _rms(mem_ref[0], g_ref[0]).astype(BF16)
    kv = _dot(mh, w_ref[0])
    k = _group_norm64(kv[:, :MEM_W], gk_ref[0]) * ATTN_SCALE
    k_ref[0, 0] = k.astype(BF16)
    v_ref[0, 0] = kv[:, MEM_W:].astype(BF16)


def _memkv(mem, mem_norm, w_kv, gk_tiled, make_riders):
    depth = w_kv.shape[0]
    b, l, d = mem.shape
    kv_out = (jax.ShapeDtypeStruct((depth, b, l, MEM_W), BF16),
              pl.BlockSpec((1, 1, l, MEM_W), lambda i, j: (i, j, 0, 0)))
    riders = make_riders(depth * b, lambda i, j: i * b + j)
    ins = [(mem, pl.BlockSpec((1, l, d), lambda i, j: (j, 0, 0))),
           (mem_norm, pl.BlockSpec((1, 1, d), lambda i, j: (i, 0, 0))),
           (w_kv, pl.BlockSpec((1, d, 2 * MEM_W), lambda i, j: (i, 0, 0))),
           (gk_tiled, pl.BlockSpec((1, 1, MEM_W), lambda i, j: (i, 0, 0)))]
    ins += [r[0] for r in riders]
    outs = [kv_out, kv_out] + [r[1] for r in riders]
    return pl.pallas_call(
        _with_riders(_memkv_kernel, 4, 2, len(riders)),
        grid=(depth, b),
        in_specs=[spec for _, spec in ins],
        out_specs=[spec for _, spec in outs],
        out_shape=[shape for shape, _ in outs],
        compiler_params=pltpu.CompilerParams(
            dimension_semantics=("arbitrary", "arbitrary"), vmem_limit_bytes=VMEM_LIMIT),
        name="memkv",
    )(*[arr for arr, _ in ins])


def _split_components(q):
    lane = lax.broadcasted_iota(jnp.int32, (1, VAL_DIM), 1)
    zero = jnp.zeros_like(q)
    return jnp.concatenate([jnp.where(lane < HEAD_DIM, q, zero),
                            jnp.where(lane >= HEAD_DIM, q, zero)], axis=0)


def _causal_keep(rows, cols):
    r = lax.broadcasted_iota(jnp.int32, (rows, cols), 0)
    c = lax.broadcasted_iota(jnp.int32, (rows, cols), 1)
    return jnp.concatenate([c <= r, c <= r], axis=0)


def _diff_combine(o2, lam_ref, sg_ref, lambda_init):
    rows = o2.shape[0] // 2
    lp = lam_ref[...]
    lam = (jnp.exp(jnp.sum(lp[0:1] * lp[1:2], axis=-1, keepdims=True))
           - jnp.exp(jnp.sum(lp[2:3] * lp[3:4], axis=-1, keepdims=True)) + lambda_init)
    o = o2[:rows] - lam * o2[rows:]
    return (_rms(o, sg_ref[...]) * (1.0 - lambda_init)).astype(BF16)


def _diff_attn_online_kernel(lam_ref, sg_ref, q_ref, k_ref, v_ref, o_ref, *, lambda_init):
    qi = pl.program_id(2)
    qq = _split_components(q_ref[0])

    def scores(j):
        rows = pl.ds(pl.multiple_of(j * ATTN_TK, ATTN_TK), ATTN_TK)
        return _dot_nt(qq, k_ref[0, rows, :]), v_ref[0, rows, :]

    s, vb = scores(qi)
    s = jnp.where(_causal_keep(ATTN_TQ, ATTN_TK), s, -jnp.inf)
    m = jnp.max(s, axis=-1, keepdims=True)
    p = jnp.exp2(s - m)
    l = jnp.sum(p, axis=-1, keepdims=True)
    acc = _dot(p.astype(BF16), vb)

    def body(j, carry):
        m, l, acc = carry
        s, vb = scores(j)
        m_new = jnp.maximum(m, jnp.max(s, axis=-1, keepdims=True))
        alpha = jnp.exp2(m - m_new)
        p = jnp.exp2(s - m_new)
        l = alpha * l + jnp.sum(p, axis=-1, keepdims=True)
        acc = alpha * acc + _dot(p.astype(BF16), vb)
        return m_new, l, acc

    m, l, acc = lax.fori_loop(0, qi, body, (m, l, acc))
    o_ref[0] = _diff_combine(acc / l, lam_ref, sg_ref, lambda_init)


def _diff_attn_bounded_kernel(lam_ref, sg_ref, q_ref, k_ref, v_ref, o_ref, vext_ref,
                              *, lambda_init):
    s_len = q_ref.shape[1]
    row = lax.broadcasted_iota(jnp.int32, (VT_ROWS - VAL_DIM, s_len), 0)
    ones_row = jnp.where(row == 0, 1.0, 0.0).astype(BF16)
    bq = ATTN_BQ
    key_i = lax.broadcasted_iota(jnp.int32, (bq, bq), 0)
    qry_i = lax.broadcasted_iota(jnp.int32, (bq, bq), 1)
    keep = jnp.concatenate([key_i <= qry_i, key_i <= qry_i], axis=1)
    lp = lam_ref[...]
    lam = (jnp.exp(jnp.sum(lp[0:1] * lp[1:2], axis=-1, keepdims=True))
           - jnp.exp(jnp.sum(lp[2:3] * lp[3:4], axis=-1, keepdims=True)) + lambda_init)
    gain_t = sg_ref[...] * (1.0 - lambda_init)

    for hh in range(q_ref.shape[2] // VAL_DIM):
        hl = slice(hh * VAL_DIM, (hh + 1) * VAL_DIM)
        r0 = hh * VT_ROWS
        vext_ref[r0:r0 + VAL_DIM, :] = v_ref[0, :, hl].T
        vext_ref[r0 + VAL_DIM:r0 + VT_ROWS, :] = ones_row
        for i in range(s_len // bq):
            lo, hi = i * bq, (i + 1) * bq
            qq = _split_components(q_ref[0, lo:hi, hl])
            t = _dot_nt(k_ref[0, :hi, hl], qq)
            p = jnp.exp2(jnp.where(keep, t[lo:], -jnp.inf)).astype(BF16)
            if i > 0:
                p = jnp.concatenate([jnp.exp2(t[:lo]).astype(BF16), p], axis=0)
            acc = _dot(vext_ref[r0:r0 + VT_ROWS, :hi], p)
            o2 = acc[:VAL_DIM] * (1.0 / acc[VAL_DIM:VAL_DIM + 1])
            o = o2[:, :bq] - lam * o2[:, bq:]
            ms = jnp.mean(o * o, axis=0, keepdims=True)
            y = o * lax.rsqrt(ms + EPS) * gain_t
            o_ref[0, lo:hi, hl] = y.T.astype(BF16)


def _diff_attn(q, k, v, lam_p, subln_g, lambda_init, logit_bound):
    _, b, s, _ = q.shape
    out_shape = jax.ShapeDtypeStruct(q.shape, BF16)

    def bounded(q, k, v):
        head = pl.BlockSpec((None, 1, s, GROUP_W), lambda bi, h: (h, bi, 0, 0))
        return pl.pallas_call(
            functools.partial(_diff_attn_bounded_kernel, lambda_init=lambda_init),
            grid=(b, HEAD_GROUPS),
            in_specs=[_const_spec(lam_p.shape), _const_spec((VAL_DIM, 1)), head, head, head],
            out_specs=head,
            out_shape=out_shape,
            scratch_shapes=[pltpu.VMEM((ATTN_HEADS_PER_STEP * VT_ROWS, s), BF16)],
            compiler_params=pltpu.CompilerParams(
                dimension_semantics=("arbitrary", "arbitrary"), vmem_limit_bytes=VMEM_LIMIT),
            name="diff_attn_bounded",
        )(lam_p, subln_g.reshape(VAL_DIM, 1), q, k, v)

    def online(q, k, v):
        hps = ATTN_HEADS_PER_STEP
        head = pl.BlockSpec((None, 1, s, VAL_DIM), lambda bi, h, i: (h // hps, bi, 0, h % hps))
        blk = pl.BlockSpec((None, 1, ATTN_TQ, VAL_DIM),
                           lambda bi, h, i: (h // hps, bi, i, h % hps))
        return pl.pallas_call(
            functools.partial(_diff_attn_online_kernel, lambda_init=lambda_init),
            grid=(b, DIFF_HEADS, s // ATTN_TQ),
            in_specs=[_const_spec(lam_p.shape), _const_spec((1, VAL_DIM)), blk, head, head],
            out_specs=blk,
            out_shape=out_shape,
            compiler_params=pltpu.CompilerParams(
                dimension_semantics=("arbitrary", "arbitrary", "arbitrary"),
                vmem_limit_bytes=VMEM_LIMIT),
            name="diff_attn_online",
        )(lam_p, subln_g, q, k, v)

    return lax.cond(logit_bound <= FAST_MAX_LOG2, bounded, online, q, k, v)


def _tile_gain(g, reps):
    return jnp.tile(g.reshape(1, -1), (1, reps))


def kernel(x, mem, ffn_norm, ffn_w_in, ffn_w_out, mix_norm, mem_norm, w_mem_kv,
           memq_norm, memk_norm, w_out, a_w_in, a_v_norm, a_w_s, a_b_s,
           b_w_in, b_q_norm, b_k_norm, b_lambda, b_subln):
    batch, seq, d = x.shape
    depth = ffn_norm.shape[0]
    x2d = x.reshape(batch * seq, d)

    n_steps = (batch * seq) // TM_FFN
    ffn_order = [(i, half) for i in range(depth) for half in range(2)]
    ffn_gain = ffn_norm.reshape(depth, 2, 1, d)

    gk_mem = jnp.tile(memk_norm.reshape(depth, 1, HEAD_DIM), (1, 1, MEM_HEADS))
    kmem, vmem, w_in_bf, w_out_bf = _memkv(
        mem, mem_norm.reshape(depth, 1, d), w_mem_kv.astype(BF16), gk_mem,
        lambda steps, flat: _ffn_weight_riders(ffn_w_in, ffn_w_out, ffn_order[0], steps, flat))

    def ffn(x2d, pos, w_in_bf, w_out_bf, extra_riders=()):
        riders = []
        if pos + 1 < len(ffn_order):
            riders = _ffn_weight_riders(ffn_w_in, ffn_w_out, ffn_order[pos + 1], n_steps,
                                        lambda i: i)
        n_next = len(riders)
        res = _ffn(x2d, _layer_const(ffn_gain, ffn_order[pos]), w_in_bf, w_out_bf,
                   riders + list(extra_riders))
        nxt = res[1:1 + n_next] if n_next else [None, None]
        return res[0], nxt[0], nxt[1], res[1 + n_next:]

    for i in range(depth):
        j = i // N_MIXERS
        g_mix = _const(mix_norm[i].reshape(1, d))
        gq_mem = _const(_tile_gain(memq_norm[i], MEM_HEADS))
        tm = TM_GMLP if i % N_MIXERS == 0 else TM_MIX
        k_i, v_i = _mem_spec(kmem, i, seq, tm), _mem_spec(vmem, i, seq, tm)
        w_mix_in = a_w_in if i % N_MIXERS == 0 else b_w_in
        mix_riders = [_cast_rider(w_mix_in, (j,), n_steps, lambda s: s),
                      _cast_rider(w_out, (i,), n_steps, lambda s: s)]
        x2d, w_in_bf, w_out_bf, (w_mix_in_bf, w_o_bf) = ffn(x2d, 2 * i, w_in_bf, w_out_bf,
                                                            mix_riders)
        w_o = _const(w_o_bf)
        if i % N_MIXERS == 0:
            mix = [g_mix, _const(w_mix_in_bf), _const(a_v_norm[j].reshape(1, TOK_W)),
                   _const(a_w_s[j]), _const(a_b_s[j].T), gq_mem]
            x2d = _gmlp_mix(x2d, mix, k_i, v_i, w_o)
        else:
            lambda_init = 0.8 - 0.6 * math.exp(-0.3 * i)
            mix = [g_mix, _const(w_mix_in_bf),
                   _const(_tile_gain(b_q_norm[j], 2 * DIFF_HEADS)),
                   _const(_tile_gain(b_k_norm[j], 2 * DIFF_HEADS)), gq_mem]
            q, k, v, qm = _diff_inproj(x2d, mix)
            logit_bound = (1.01 * HEAD_DIM * ATTN_SCALE * LOG2E
                           * jnp.max(jnp.abs(b_q_norm[j])) * jnp.max(jnp.abs(b_k_norm[j])))
            per_row = (HEAD_GROUPS, batch, seq, GROUP_W)
            tok = _diff_attn(q.reshape(per_row), k.reshape(per_row), v.reshape(per_row),
                             b_lambda[j], b_subln[j].reshape(1, VAL_DIM), lambda_init,
                             logit_bound)
            x2d = _diff_outproj(x2d, tok.reshape(HEAD_GROUPS, batch * seq, GROUP_W), qm,
                                k_i, v_i, w_o)
        x2d, w_in_bf, w_out_bf, _ = ffn(x2d, 2 * i + 1, w_in_bf, w_out_bf)
    return x2d.reshape(batch, seq, d)
```

```python
import functools
import math

import jax
import jax.numpy as jnp
from jax import lax
from jax.experimental import pallas as pl
from jax.experimental.pallas import tpu as pltpu

D_MODEL = 1024
HEAD_DIM = 64
MEM_HEADS = 4
MEM_W = MEM_HEADS * HEAD_DIM
TOK_W = D_MODEL - MEM_W
CHUNK = 128
GROUP_DIM = 128
GROUPS = TOK_W // GROUP_DIM
DIFF_HEADS = TOK_W // (2 * HEAD_DIM)
VAL_DIM = 2 * HEAD_DIM
N_MIXERS = 2
EPS = 1e-6
ATTN_SCALE = HEAD_DIM ** -0.5
LOG2E = math.log2(math.e)
FAST_MAX_LOG2 = 60.0

LANES = 128
BF16_ROWS = 16
V7X_VMEM_BYTES = 64 * 1024 * 1024

SUB_FFN = 256
SUB_GMLP = 256
SUB_MIX = 512
TM_FFN = 1024
TM_GMLP = 1024
TM_MIX = 2048
ATTN_TQ = 256
ATTN_TK = 256
ATTN_BQ = 512
ATTN_HEADS_PER_STEP = 3
HEAD_GROUPS = DIFF_HEADS // ATTN_HEADS_PER_STEP
GROUP_W = ATTN_HEADS_PER_STEP * VAL_DIM
VT_ROWS = VAL_DIM + BF16_ROWS
VMEM_LIMIT = V7X_VMEM_BYTES * 7 // 8

F32 = jnp.float32
BF16 = jnp.bfloat16


def _const_spec(shape):
    nd = len(shape)
    return pl.BlockSpec(shape, lambda *_: (0,) * nd, pipeline_mode=pl.Buffered(1))


def _rms(x, g):
    ms = jnp.mean(x * x, axis=-1, keepdims=True)
    return x * lax.rsqrt(ms + EPS) * g


def _dot(a, b):
    return jnp.dot(a, b, preferred_element_type=F32)


def _dot_nt(a, b):
    return lax.dot_general(a, b, (((1,), (1,)), ((), ())), preferred_element_type=F32)


def _group_norm64(x, g):
    low = lax.broadcasted_iota(jnp.int32, (1, LANES), 1) < HEAD_DIM
    outs = []
    for j in range(x.shape[-1] // LANES):
        xs = x[:, j * LANES:(j + 1) * LANES]
        sq = xs * xs
        s_lo = jnp.sum(jnp.where(low, sq, 0.0), axis=-1, keepdims=True)
        s_hi = jnp.sum(jnp.where(low, 0.0, sq), axis=-1, keepdims=True)
        r_lo = lax.rsqrt(s_lo * (1.0 / HEAD_DIM) + EPS)
        r_hi = lax.rsqrt(s_hi * (1.0 / HEAD_DIM) + EPS)
        outs.append(xs * jnp.where(low, r_lo, r_hi))
    return jnp.concatenate(outs, axis=-1) * g


def _skewed_trace(stages, tiles):
    states = [{} for _ in tiles]
    for t in range(len(tiles) + len(stages) - 1):
        for j in range(len(tiles)):
            if 0 <= t - j < len(stages):
                stages[t - j](states[j], tiles[j])


def _gelu_exact(x):
    return 0.5 * x * (1.0 + lax.erf(x * math.sqrt(0.5)))


def _read(ref, sub):
    return lambda st, r0: ref[r0:r0 + sub, :]


def _write(ref, sub):
    def write(st, r0, val):
        ref[r0:r0 + sub, :] = val
    return write


def _ffn_stages(read_x, write_out, g_ref, win_ref, wout_ref):
    d_ff = wout_ref.shape[0]

    def up_proj(st, r0):
        st["ffn_x"] = read_x(st, r0)
        h = _rms(st["ffn_x"], g_ref[...]).astype(BF16)
        st["gate_up"] = _dot(h, win_ref[...])

    def activate(st, r0):
        gate_up = st.pop("gate_up")
        gate, up = gate_up[:, :d_ff], gate_up[:, d_ff:]
        st["act"] = (gate * jax.nn.sigmoid(gate) * up).astype(BF16)

    def down_proj(st, r0):
        write_out(st, r0, st.pop("ffn_x") + 0.5 * _dot(st.pop("act"), wout_ref[...]))

    return [up_proj, activate, down_proj]


def _mem_attention(qn, k_ref, v_ref):
    rows = qn.shape[0]
    lane = lax.broadcasted_iota(jnp.int32, (1, MEM_W), 1)
    sels = [(lane >= h * HEAD_DIM) & (lane < (h + 1) * HEAD_DIM) for h in range(MEM_HEADS)]
    qs = jnp.concatenate([jnp.where(sel, qn, jnp.zeros_like(qn)) for sel in sels], axis=0)
    s = _dot_nt(qs, k_ref[0])
    p = jnp.exp(s - jnp.max(s, axis=-1, keepdims=True))
    l = jnp.sum(p, axis=-1, keepdims=True)
    o = _dot(p.astype(BF16), v_ref[0]) / l
    out = jnp.zeros((rows, MEM_W), F32)
    for h, sel in enumerate(sels):
        out = out + jnp.where(sel, o[h * rows:(h + 1) * rows], 0.0)
    return out


def _gmlp_stages(read_x, write_out, g_ref, win_ref, vg_ref, ws_ref, bst_ref, gq_ref,
                 k_ref, v_ref, wout_ref, sub):
    n_chunks = sub // CHUNK
    row = lax.broadcasted_iota(jnp.int32, (CHUNK, CHUNK), 0)
    col = lax.broadcasted_iota(jnp.int32, (CHUNK, CHUNK), 1)
    ws = [jnp.where(col <= row, ws_ref[g], 0.0).astype(BF16) for g in range(GROUPS)]
    bst = bst_ref[...]

    def in_proj(st, r0):
        st["mix_x"] = read_x(st, r0)
        h = _rms(st["mix_x"], g_ref[...]).astype(BF16)
        st["z"] = _dot(h, win_ref[...])

    def gate(st, r0):
        st["u"] = _gelu_exact(st.pop("z")[:, :TOK_W])

    def activate(st, r0):
        z = st["z"]
        vact = _gelu_exact(z[:, TOK_W:2 * TOK_W])
        st["vn"] = [_rms(vact[:, g * GROUP_DIM:(g + 1) * GROUP_DIM],
                         vg_ref[:, g * GROUP_DIM:(g + 1) * GROUP_DIM]).astype(BF16)
                    for g in range(GROUPS)]
        st["qn"] = _group_norm64(z[:, 2 * TOK_W:], gq_ref[...]).astype(BF16)

    def spatial(st, r0):
        u, vn = st.pop("u"), st.pop("vn")
        tok_cols = []
        for g in range(GROUPS):
            vg = jnp.concatenate([vn[g][n * CHUNK:(n + 1) * CHUNK] for n in range(n_chunks)],
                                 axis=1)
            mixed = _dot(ws[g], vg) + bst[:, g:g + 1]
            mixed = jnp.concatenate(
                [mixed[:, n * GROUP_DIM:(n + 1) * GROUP_DIM] for n in range(n_chunks)], axis=0)
            tok_cols.append(u[:, g * GROUP_DIM:(g + 1) * GROUP_DIM] * mixed)
        st["tok"] = jnp.concatenate(tok_cols, axis=-1).astype(BF16)

    def mem_attn(st, r0):
        st["mo"] = _mem_attention(st.pop("qn"), k_ref, v_ref).astype(BF16)

    def out_proj(st, r0):
        y = _dot(jnp.concatenate([st.pop("tok"), st.pop("mo")], axis=1), wout_ref[...])
        write_out(st, r0, st.pop("mix_x") + y)

    return [in_proj, activate, gate, spatial, mem_attn, out_proj]


def _diff_inproj_stages(read_x, g_ref, win_ref, gq_ref, gk_ref, gqm_ref,
                        q_ref, k_ref, v_ref, qm_ref, sub):
    def in_proj(st, r0):
        h = _rms(read_x(st, r0), g_ref[...]).astype(BF16)
        st["z"] = _dot(h, win_ref[...])

    def normalise(st, r0):
        rs = slice(r0, r0 + sub)
        z = st.pop("z")
        q = _group_norm64(z[:, :TOK_W], gq_ref[...]) * (ATTN_SCALE * LOG2E)
        k = _group_norm64(z[:, TOK_W:2 * TOK_W], gk_ref[...])
        v = z[:, 2 * TOK_W:3 * TOK_W]
        for g in range(HEAD_GROUPS):
            cs = slice(g * GROUP_W, (g + 1) * GROUP_W)
            q_ref[g, rs, :] = q[:, cs].astype(BF16)
            k_ref[g, rs, :] = k[:, cs].astype(BF16)
            v_ref[g, rs, :] = v[:, cs].astype(BF16)
        qm_ref[rs, :] = _group_norm64(z[:, 3 * TOK_W:], gqm_ref[...]).astype(BF16)

    return [in_proj, normalise]


def _diff_outproj_stages(read_x, write_out, tok_ref, qm_ref, k_ref, v_ref, wout_ref, sub):
    def mem_attn(st, r0):
        st["mo"] = _mem_attention(qm_ref[r0:r0 + sub, :], k_ref, v_ref).astype(BF16)

    def out_proj(st, r0):
        cat = [tok_ref[g, r0:r0 + sub, :] for g in range(HEAD_GROUPS)] + [st.pop("mo")]
        y = _dot(jnp.concatenate(cat, axis=1), wout_ref[...])
        write_out(st, r0, read_x(st, r0) + y)

    return [mem_attn, out_proj]


def _run_stages(stages, rows, sub):
    _skewed_trace(stages, list(range(0, rows, sub)))


def _ffn_kernel(x_ref, fg_ref, fwin_ref, fwout_ref, o_ref):
    sub = SUB_FFN
    _run_stages(_ffn_stages(_read(x_ref, sub), _write(o_ref, sub), fg_ref, fwin_ref, fwout_ref),
                x_ref.shape[0], sub)


def _gmlp_kernel(x_ref, g_ref, win_ref, vg_ref, ws_ref, bst_ref, gq_ref, k_ref, v_ref,
                 wout_ref, o_ref):
    sub = SUB_GMLP
    _run_stages(_gmlp_stages(_read(x_ref, sub), _write(o_ref, sub), g_ref, win_ref, vg_ref,
                             ws_ref, bst_ref, gq_ref, k_ref, v_ref, wout_ref, sub),
                x_ref.shape[0], sub)


def _diff_inproj_kernel(x_ref, g_ref, win_ref, gq_ref, gk_ref, gqm_ref,
                        q_ref, k_ref, v_ref, qm_ref):
    sub = SUB_MIX
    _run_stages(_diff_inproj_stages(_read(x_ref, sub), g_ref, win_ref, gq_ref, gk_ref, gqm_ref,
                                    q_ref, k_ref, v_ref, qm_ref, sub), x_ref.shape[0], sub)


def _diff_outproj_kernel(x_ref, tok_ref, qm_ref, k_ref, v_ref, wout_ref, o_ref):
    sub = SUB_MIX
    _run_stages(_diff_outproj_stages(_read(x_ref, sub), _write(o_ref, sub), tok_ref, qm_ref,
                                     k_ref, v_ref, wout_ref, sub), x_ref.shape[0], sub)


def _cast_blocks(rows, n_steps):
    for blocks in range(n_steps, 0, -1):
        if n_steps % blocks == 0 and rows % (blocks * BF16_ROWS) == 0:
            return blocks
    raise ValueError(f"{rows} rows cannot be split into bf16 row blocks over {n_steps} steps")


def _cast_rider(stacked, lead, n_steps, flat_step):
    rows, cols = stacked.shape[len(lead):]
    blocks = _cast_blocks(rows, n_steps)
    steps_per_block = n_steps // blocks
    squeezed = (None,) * len(lead)

    def src_index(*g):
        return lead + (flat_step(*g) // steps_per_block, 0)

    def dst_index(*g):
        return (flat_step(*g) // steps_per_block, 0)

    src = pl.BlockSpec(squeezed + (rows // blocks, cols), src_index)
    dst = pl.BlockSpec((rows // blocks, cols), dst_index)
    return (stacked, src), (jax.ShapeDtypeStruct((rows, cols), BF16), dst)


def _with_riders(kernel_fn, n_in, n_out, n_riders):
    def kern(*refs):
        ins, rin = refs[:n_in], refs[n_in:n_in + n_riders]
        outs = refs[n_in + n_riders:n_in + n_riders + n_out]
        rout = refs[n_in + n_riders + n_out:n_in + 2 * n_riders + n_out]
        for src, dst in zip(rin, rout):
            dst[...] = src[...].astype(BF16)
        kernel_fn(*ins, *outs, *refs[n_in + 2 * n_riders + n_out:])
    return kern


def _ffn_weight_riders(ffn_w_in, ffn_w_out, lead, n_steps, flat_step):
    return [_cast_rider(ffn_w_in, lead, n_steps, flat_step),
            _cast_rider(ffn_w_out, lead, n_steps, flat_step)]


def _row_spec(tm, width):
    return pl.BlockSpec((tm, width), lambda i: (i, 0))


def _group_row_spec(tm):
    return pl.BlockSpec((HEAD_GROUPS, tm, GROUP_W), lambda i: (0, i, 0))


def _layer_const(stacked, lead):
    tail = stacked.shape[len(lead):]
    spec = pl.BlockSpec((None,) * len(lead) + tail, lambda *_: lead + (0,) * len(tail),
                        pipeline_mode=pl.Buffered(1))
    return (stacked, spec)


def _mem_spec(stacked, layer, seq, tm):
    tiles_per_row = seq // tm
    _, _, l, w = stacked.shape
    return (stacked, pl.BlockSpec((None, 1, l, w), lambda i: (layer, i // tiles_per_row, 0, 0)))


def _const(arr):
    return (arr, _const_spec(arr.shape))


def _token_call(kernel_fn, name, n, tm, ins, outs, riders=()):
    rin = [r[0] for r in riders]
    rout = [r[1] for r in riders]
    if riders:
        kernel_fn = _with_riders(kernel_fn, len(ins), len(outs), len(riders))
    return pl.pallas_call(
        kernel_fn,
        grid=(n // tm,),
        in_specs=[spec for _, spec in ins + rin],
        out_specs=[spec for _, spec in outs + rout],
        out_shape=[shape for shape, _ in outs + rout],
        compiler_params=pltpu.CompilerParams(
            dimension_semantics=("arbitrary",), vmem_limit_bytes=VMEM_LIMIT),
        name=name,
    )(*[arr for arr, _ in ins + rin])


def _ffn(x2d, gain, w_in, w_out, riders):
    n, d = x2d.shape
    return _token_call(
        _ffn_kernel, "ffn", n, TM_FFN,
        [(x2d, _row_spec(TM_FFN, d)), gain, _const(w_in), _const(w_out)],
        [(jax.ShapeDtypeStruct((n, d), F32), _row_spec(TM_FFN, d))], riders)


def _gmlp_mix(x2d, mix, kmem, vmem, w_o):
    n, d = x2d.shape
    (out,) = _token_call(
        _gmlp_kernel, "gmlp_mix", n, TM_GMLP,
        [(x2d, _row_spec(TM_GMLP, d))] + mix + [kmem, vmem, w_o],
        [(jax.ShapeDtypeStruct((n, d), F32), _row_spec(TM_GMLP, d))])
    return out


def _diff_inproj(x2d, mix):
    n, d = x2d.shape
    grouped = (jax.ShapeDtypeStruct((HEAD_GROUPS, n, GROUP_W), BF16), _group_row_spec(TM_MIX))
    return _token_call(
        _diff_inproj_kernel, "diff_inproj", n, TM_MIX, [(x2d, _row_spec(TM_MIX, d))] + mix,
        [grouped, grouped, grouped,
         (jax.ShapeDtypeStruct((n, MEM_W), BF16), _row_spec(TM_MIX, MEM_W))])


def _diff_outproj(x2d, tok, qm, kmem, vmem, w_o):
    n, d = x2d.shape
    (out,) = _token_call(
        _diff_outproj_kernel, "diff_outproj", n, TM_MIX,
        [(x2d, _row_spec(TM_MIX, d)), (tok, _group_row_spec(TM_MIX)),
         (qm, _row_spec(TM_MIX, MEM_W)), kmem, vmem, w_o],
        [(jax.ShapeDtypeStruct((n, d), F32), _row_spec(TM_MIX, d))])
    return out


def _memkv_kernel(mem_ref, g_ref, w_ref, gk_ref, k_ref, v_ref):
    mh = _rms(mem_ref[0], g_ref[0]).astype(BF16)
    kv = _dot(mh, w_ref[0])
    k = _group_norm64(kv[:, :MEM_W], gk_ref[0]) * ATTN_SCALE
    k_ref[0, 0] = k.astype(BF16)
    v_ref[0, 0] = kv[:, MEM_W:].astype(BF16)


def _memkv(mem, mem_norm, w_kv, gk_tiled, make_riders):
    depth = w_kv.shape[0]
    b, l, d = mem.shape
    kv_out = (jax.ShapeDtypeStruct((depth, b, l, MEM_W), BF16),
              pl.BlockSpec((1, 1, l, MEM_W), lambda i, j: (i, j, 0, 0)))
    riders = make_riders(depth * b, lambda i, j: i * b + j)
    ins = [(mem, pl.BlockSpec((1, l, d), lambda i, j: (j, 0, 0))),
           (mem_norm, pl.BlockSpec((1, 1, d), lambda i, j: (i, 0, 0))),
           (w_kv, pl.BlockSpec((1, d, 2 * MEM_W), lambda i, j: (i, 0, 0))),
           (gk_tiled, pl.BlockSpec((1, 1, MEM_W), lambda i, j: (i, 0, 0)))]
    ins += [r[0] for r in riders]
    outs = [kv_out, kv_out] + [r[1] for r in riders]
    return pl.pallas_call(
        _with_riders(_memkv_kernel, 4, 2, len(riders)),
        grid=(depth, b),
        in_specs=[spec for _, spec in ins],
        out_specs=[spec for _, spec in outs],
        out_shape=[shape for shape, _ in outs],
        compiler_params=pltpu.CompilerParams(
            dimension_semantics=("arbitrary", "arbitrary"), vmem_limit_bytes=VMEM_LIMIT),
        name="memkv",
    )(*[arr for arr, _ in ins])


def _split_components(q):
    lane = lax.broadcasted_iota(jnp.int32, (1, VAL_DIM), 1)
    zero = jnp.zeros_like(q)
    return jnp.concatenate([jnp.where(lane < HEAD_DIM, q, zero),
                            jnp.where(lane >= HEAD_DIM, q, zero)], axis=0)


def _causal_keep(rows, cols):
    r = lax.broadcasted_iota(jnp.int32, (rows, cols), 0)
    c = lax.broadcasted_iota(jnp.int32, (rows, cols), 1)
    return jnp.concatenate([c <= r, c <= r], axis=0)


def _diff_combine(o2, lam_ref, sg_ref, lambda_init):
    rows = o2.shape[0] // 2
    lp = lam_ref[...]
    lam = (jnp.exp(jnp.sum(lp[0:1] * lp[1:2], axis=-1, keepdims=True))
           - jnp.exp(jnp.sum(lp[2:3] * lp[3:4], axis=-1, keepdims=True)) + lambda_init)
    o = o2[:rows] - lam * o2[rows:]
    return (_rms(o, sg_ref[...]) * (1.0 - lambda_init)).astype(BF16)


def _diff_attn_online_kernel(lam_ref, sg_ref, q_ref, k_ref, v_ref, o_ref, *, lambda_init):
    qi = pl.program_id(2)
    qq = _split_components(q_ref[0])

    def scores(j):
        rows = pl.ds(pl.multiple_of(j * ATTN_TK, ATTN_TK), ATTN_TK)
        return _dot_nt(qq, k_ref[0, rows, :]), v_ref[0, rows, :]

    s, vb = scores(qi)
    s = jnp.where(_causal_keep(ATTN_TQ, ATTN_TK), s, -jnp.inf)
    m = jnp.max(s, axis=-1, keepdims=True)
    p = jnp.exp2(s - m)
    l = jnp.sum(p, axis=-1, keepdims=True)
    acc = _dot(p.astype(BF16), vb)

    def body(j, carry):
        m, l, acc = carry
        s, vb = scores(j)
        m_new = jnp.maximum(m, jnp.max(s, axis=-1, keepdims=True))
        alpha = jnp.exp2(m - m_new)
        p = jnp.exp2(s - m_new)
        l = alpha * l + jnp.sum(p, axis=-1, keepdims=True)
        acc = alpha * acc + _dot(p.astype(BF16), vb)
        return m_new, l, acc

    m, l, acc = lax.fori_loop(0, qi, body, (m, l, acc))
    o_ref[0] = _diff_combine(acc / l, lam_ref, sg_ref, lambda_init)


def _diff_attn_bounded_kernel(lam_ref, sg_ref, q_ref, k_ref, v_ref, o_ref, vext_ref,
                              *, lambda_init):
    s_len = q_ref.shape[1]
    row = lax.broadcasted_iota(jnp.int32, (VT_ROWS - VAL_DIM, s_len), 0)
    ones_row = jnp.where(row == 0, 1.0, 0.0).astype(BF16)
    bq = ATTN_BQ
    key_i = lax.broadcasted_iota(jnp.int32, (bq, bq), 0)
    qry_i = lax.broadcasted_iota(jnp.int32, (bq, bq), 1)
    keep = jnp.concatenate([key_i <= qry_i, key_i <= qry_i], axis=1)
    lp = lam_ref[...]
    lam = (jnp.exp(jnp.sum(lp[0:1] * lp[1:2], axis=-1, keepdims=True))
           - jnp.exp(jnp.sum(lp[2:3] * lp[3:4], axis=-1, keepdims=True)) + lambda_init)
    gain_t = sg_ref[...] * (1.0 - lambda_init)

    for hh in range(q_ref.shape[2] // VAL_DIM):
        hl = slice(hh * VAL_DIM, (hh + 1) * VAL_DIM)
        r0 = hh * VT_ROWS
        vext_ref[r0:r0 + VAL_DIM, :] = v_ref[0, :, hl].T
        vext_ref[r0 + VAL_DIM:r0 + VT_ROWS, :] = ones_row
        for i in range(s_len // bq):
            lo, hi = i * bq, (i + 1) * bq
            qq = _split_components(q_ref[0, lo:hi, hl])
            t = _dot_nt(k_ref[0, :hi, hl], qq)
            p = jnp.exp2(jnp.where(keep, t[lo:], -jnp.inf)).astype(BF16)
            if i > 0:
                p = jnp.concatenate([jnp.exp2(t[:lo]).astype(BF16), p], axis=0)
            acc = _dot(vext_ref[r0:r0 + VT_ROWS, :hi], p)
            o2 = acc[:VAL_DIM] * (1.0 / acc[VAL_DIM:VAL_DIM + 1])
            o = o2[:, :bq] - lam * o2[:, bq:]
            ms = jnp.mean(o * o, axis=0, keepdims=True)
            y = o * lax.rsqrt(ms + EPS) * gain_t
            o_ref[0, lo:hi, hl] = y.T.astype(BF16)


def _diff_attn(q, k, v, lam_p, subln_g, lambda_init, logit_bound):
    _, b, s, _ = q.shape
    out_shape = jax.ShapeDtypeStruct(q.shape, BF16)

    def bounded(q, k, v):
        head = pl.BlockSpec((None, 1, s, GROUP_W), lambda bi, h: (h, bi, 0, 0))
        return pl.pallas_call(
            functools.partial(_diff_attn_bounded_kernel, lambda_init=lambda_init),
            grid=(b, HEAD_GROUPS),
            in_specs=[_const_spec(lam_p.shape), _const_spec((VAL_DIM, 1)), head, head, head],
            out_specs=head,
            out_shape=out_shape,
            scratch_shapes=[pltpu.VMEM((ATTN_HEADS_PER_STEP * VT_ROWS, s), BF16)],
            compiler_params=pltpu.CompilerParams(
                dimension_semantics=("arbitrary", "arbitrary"), vmem_limit_bytes=VMEM_LIMIT),
            name="diff_attn_bounded",
        )(lam_p, subln_g.reshape(VAL_DIM, 1), q, k, v)

    def online(q, k, v):
        hps = ATTN_HEADS_PER_STEP
        head = pl.BlockSpec((None, 1, s, VAL_DIM), lambda bi, h, i: (h // hps, bi, 0, h % hps))
        blk = pl.BlockSpec((None, 1, ATTN_TQ, VAL_DIM),
                           lambda bi, h, i: (h // hps, bi, i, h % hps))
        return pl.pallas_call(
            functools.partial(_diff_attn_online_kernel, lambda_init=lambda_init),
            grid=(b, DIFF_HEADS, s // ATTN_TQ),
            in_specs=[_const_spec(lam_p.shape), _const_spec((1, VAL_DIM)), blk, head, head],
            out_specs=blk,
            out_shape=out_shape,
            compiler_params=pltpu.CompilerParams(
                dimension_semantics=("arbitrary", "arbitrary", "arbitrary"),
                vmem_limit_bytes=VMEM_LIMIT),
            name="diff_attn_online",
        )(lam_p, subln_g, q, k, v)

    return lax.cond(logit_bound <= FAST_MAX_LOG2, bounded, online, q, k, v)


def _tile_gain(g, reps):
    return jnp.tile(g.reshape(1, -1), (1, reps))


def kernel(x, mem, ffn_norm, ffn_w_in, ffn_w_out, mix_norm, mem_norm, w_mem_kv,
           memq_norm, memk_norm, w_out, a_w_in, a_v_norm, a_w_s, a_b_s,
           b_w_in, b_q_norm, b_k_norm, b_lambda, b_subln):
    batch, seq, d = x.shape
    depth = ffn_norm.shape[0]
    x2d = x.reshape(batch * seq, d)

    n_steps = (batch * seq) // TM_FFN
    ffn_order = [(i, half) for i in range(depth) for half in range(2)]
    ffn_gain = ffn_norm.reshape(depth, 2, 1, d)

    gk_mem = jnp.tile(memk_norm.reshape(depth, 1, HEAD_DIM), (1, 1, MEM_HEADS))
    kmem, vmem, w_in_bf, w_out_bf = _memkv(
        mem, mem_norm.reshape(depth, 1, d), w_mem_kv.astype(BF16), gk_mem,
        lambda steps, flat: _ffn_weight_riders(ffn_w_in, ffn_w_out, ffn_order[0], steps, flat))

    def ffn(x2d, pos, w_in_bf, w_out_bf, extra_riders=()):
        riders = []
        if pos + 1 < len(ffn_order):
            riders = _ffn_weight_riders(ffn_w_in, ffn_w_out, ffn_order[pos + 1], n_steps,
                                        lambda i: i)
        n_next = len(riders)
        res = _ffn(x2d, _layer_const(ffn_gain, ffn_order[pos]), w_in_bf, w_out_bf,
                   riders + list(extra_riders))
        nxt = res[1:1 + n_next] if n_next else [None, None]
        return res[0], nxt[0], nxt[1], res[1 + n_next:]

    for i in range(depth):
        j = i // N_MIXERS
        g_mix = _const(mix_norm[i].reshape(1, d))
        gq_mem = _const(_tile_gain(memq_norm[i], MEM_HEADS))
        tm = TM_GMLP if i % N_MIXERS == 0 else TM_MIX
        k_i, v_i = _mem_spec(kmem, i, seq, tm), _mem_spec(vmem, i, seq, tm)
        w_mix_in = a_w_in if i % N_MIXERS == 0 else b_w_in
        mix_riders = [_cast_rider(w_mix_in, (j,), n_steps, lambda s: s),
                      _cast_rider(w_out, (i,), n_steps, lambda s: s)]
        x2d, w_in_bf, w_out_bf, (w_mix_in_bf, w_o_bf) = ffn(x2d, 2 * i, w_in_bf, w_out_bf,
                                                            mix_riders)
        w_o = _const(w_o_bf)
        if i % N_MIXERS == 0:
            mix = [g_mix, _const(w_mix_in_bf), _const(a_v_norm[j].reshape(1, TOK_W)),
                   _const(a_w_s[j]), _const(a_b_s[j].T), gq_mem]
            x2d = _gmlp_mix(x2d, mix, k_i, v_i, w_o)
        else:
            lambda_init = 0.8 - 0.6 * math.exp(-0.3 * i)
            mix = [g_mix, _const(w_mix_in_bf),
                   _const(_tile_gain(b_q_norm[j], 2 * DIFF_HEADS)),
                   _const(_tile_gain(b_k_norm[j], 2 * DIFF_HEADS)), gq_mem]
            q, k, v, qm = _diff_inproj(x2d, mix)
            logit_bound = (1.01 * HEAD_DIM * ATTN_SCALE * LOG2E
                           * jnp.max(jnp.abs(b_q_norm[j])) * jnp.max(jnp.abs(b_k_norm[j])))
            per_row = (HEAD_GROUPS, batch, seq, GROUP_W)
            tok = _diff_attn(q.reshape(per_row), k.reshape(per_row), v.reshape(per_row),
                             b_lambda[j], b_subln[j].reshape(1, VAL_DIM), lambda_init,
                             logit_bound)
            x2d = _diff_outproj(x2d, tok.reshape(HEAD_GROUPS, batch * seq, GROUP_W), qm,
                                k_i, v_i, w_o)
        x2d, w_in_bf, w_out_bf, _ = ffn(x2d, 2 * i + 1, w_in_bf, w_out_bf)
    return x2d.reshape(batch, seq, d)
```

```python
import functools
import math

import jax
import jax.numpy as jnp
from jax import lax
from jax.experimental import pallas as pl
from jax.experimental.pallas import tpu as pltpu

D_MODEL = 1024
HEAD_DIM = 64
MEM_HEADS = 4
MEM_W = MEM_HEADS * HEAD_DIM
TOK_W = D_MODEL - MEM_W
CHUNK = 128
GROUP_DIM = 128
GROUPS = TOK_W // GROUP_DIM
DIFF_HEADS = TOK_W // (2 * HEAD_DIM)
VAL_DIM = 2 * HEAD_DIM
N_MIXERS = 2
EPS = 1e-6
ATTN_SCALE = HEAD_DIM ** -0.5
LOG2E = math.log2(math.e)
FAST_MAX_LOG2 = 60.0

LANES = 128
BF16_ROWS = 16
V7X_VMEM_BYTES = 64 * 1024 * 1024

SUB_FFN = 256
SUB_GMLP = 256
SUB_MIX = 512
TM_FFN = 1024
TM_GMLP = 1024
TM_MIX = 2048
ATTN_TQ = 256
ATTN_TK = 256
ATTN_BQ = 512
ATTN_HEADS_PER_STEP = 3
HEAD_GROUPS = DIFF_HEADS // ATTN_HEADS_PER_STEP
GROUP_W = ATTN_HEADS_PER_STEP * VAL_DIM
VT_ROWS = VAL_DIM + BF16_ROWS
VMEM_LIMIT = V7X_VMEM_BYTES * 7 // 8

F32 = jnp.float32
BF16 = jnp.bfloat16


def _const_spec(shape):
    nd = len(shape)
    return pl.BlockSpec(shape, lambda *_: (0,) * nd, pipeline_mode=pl.Buffered(1))


def _rms(x, g):
    ms = jnp.mean(x * x, axis=-1, keepdims=True)
    return x * lax.rsqrt(ms + EPS) * g


def _dot(a, b):
    return jnp.dot(a, b, preferred_element_type=F32)


def _dot_nt(a, b):
    return lax.dot_general(a, b, (((1,), (1,)), ((), ())), preferred_element_type=F32)


def _group_norm64(x, g):
    low = lax.broadcasted_iota(jnp.int32, (1, LANES), 1) < HEAD_DIM
    outs = []
    for j in range(x.shape[-1] // LANES):
        xs = x[:, j * LANES:(j + 1) * LANES]
        sq = xs * xs
        s_lo = jnp.sum(jnp.where(low, sq, 0.0), axis=-1, keepdims=True)
        s_hi = jnp.sum(jnp.where(low, 0.0, sq), axis=-1, keepdims=True)
        r_lo = lax.rsqrt(s_lo * (1.0 / HEAD_DIM) + EPS)
        r_hi = lax.rsqrt(s_hi * (1.0 / HEAD_DIM) + EPS)
        outs.append(xs * jnp.where(low, r_lo, r_hi))
    return jnp.concatenate(outs, axis=-1) * g


def _skewed_trace(stages, tiles):
    states = [{} for _ in tiles]
    for t in range(len(tiles) + len(stages) - 1):
        for j in range(len(tiles)):
            if 0 <= t - j < len(stages):
                stages[t - j](states[j], tiles[j])


def _gelu_exact(x):
    return 0.5 * x * (1.0 + lax.erf(x * math.sqrt(0.5)))


def _read(ref, sub):
    return lambda st, r0: ref[r0:r0 + sub, :]


def _write(ref, sub):
    def write(st, r0, val):
        ref[r0:r0 + sub, :] = val
    return write


def _ffn_stages(read_x, write_out, g_ref, win_ref, wout_ref):
    d_ff = wout_ref.shape[0]

    def up_proj(st, r0):
        st["ffn_x"] = read_x(st, r0)
        h = _rms(st["ffn_x"], g_ref[...]).astype(BF16)
        st["gate_up"] = _dot(h, win_ref[...])

    def activate(st, r0):
        gate_up = st.pop("gate_up")
        gate, up = gate_up[:, :d_ff], gate_up[:, d_ff:]
        st["act"] = (gate * jax.nn.sigmoid(gate) * up).astype(BF16)

    def down_proj(st, r0):
        write_out(st, r0, st.pop("ffn_x") + 0.5 * _dot(st.pop("act"), wout_ref[...]))

    return [up_proj, activate, down_proj]


def _mem_attention(qn, k_ref, v_ref):
    rows = qn.shape[0]
    lane = lax.broadcasted_iota(jnp.int32, (1, MEM_W), 1)
    sels = [(lane >= h * HEAD_DIM) & (lane < (h + 1) * HEAD_DIM) for h in range(MEM_HEADS)]
    qs = jnp.concatenate([jnp.where(sel, qn, jnp.zeros_like(qn)) for sel in sels], axis=0)
    s = _dot_nt(qs, k_ref[0])
    p = jnp.exp(s - jnp.max(s, axis=-1, keepdims=True))
    l = jnp.sum(p, axis=-1, keepdims=True)
    o = _dot(p.astype(BF16), v_ref[0]) / l
    out = jnp.zeros((rows, MEM_W), F32)
    for h, sel in enumerate(sels):
        out = out + jnp.where(sel, o[h * rows:(h + 1) * rows], 0.0)
    return out


def _gmlp_stages(read_x, write_out, g_ref, win_ref, vg_ref, ws_ref, bst_ref, gq_ref,
                 k_ref, v_ref, wout_ref, sub):
    n_chunks = sub // CHUNK
    row = lax.broadcasted_iota(jnp.int32, (CHUNK, CHUNK), 0)
    col = lax.broadcasted_iota(jnp.int32, (CHUNK, CHUNK), 1)
    ws = [jnp.where(col <= row, ws_ref[g], 0.0).astype(BF16) for g in range(GROUPS)]
    bst = bst_ref[...]

    def in_proj(st, r0):
        st["mix_x"] = read_x(st, r0)
        h = _rms(st["mix_x"], g_ref[...]).astype(BF16)
        st["z"] = _dot(h, win_ref[...])

    def gate(st, r0):
        st["u"] = _gelu_exact(st.pop("z")[:, :TOK_W])

    def activate(st, r0):
        z = st["z"]
        vact = _gelu_exact(z[:, TOK_W:2 * TOK_W])
        st["vn"] = [_rms(vact[:, g * GROUP_DIM:(g + 1) * GROUP_DIM],
                         vg_ref[:, g * GROUP_DIM:(g + 1) * GROUP_DIM]).astype(BF16)
                    for g in range(GROUPS)]
        st["qn"] = _group_norm64(z[:, 2 * TOK_W:], gq_ref[...]).astype(BF16)

    def spatial(st, r0):
        u, vn = st.pop("u"), st.pop("vn")
        tok_cols = []
        for g in range(GROUPS):
            vg = jnp.concatenate([vn[g][n * CHUNK:(n + 1) * CHUNK] for n in range(n_chunks)],
                                 axis=1)
            mixed = _dot(ws[g], vg) + bst[:, g:g + 1]
            mixed = jnp.concatenate(
                [mixed[:, n * GROUP_DIM:(n + 1) * GROUP_DIM] for n in range(n_chunks)], axis=0)
            tok_cols.append(u[:, g * GROUP_DIM:(g + 1) * GROUP_DIM] * mixed)
        st["tok"] = jnp.concatenate(tok_cols, axis=-1).astype(BF16)

    def mem_attn(st, r0):
        st["mo"] = _mem_attention(st.pop("qn"), k_ref, v_ref).astype(BF16)

    def out_proj(st, r0):
        y = _dot(jnp.concatenate([st.pop("tok"), st.pop("mo")], axis=1), wout_ref[...])
        write_out(st, r0, st.pop("mix_x") + y)

    return [in_proj, activate, gate, spatial, mem_attn, out_proj]


def _diff_inproj_stages(read_x, g_ref, win_ref, gq_ref, gk_ref, gqm_ref,
                        q_ref, k_ref, v_ref, qm_ref, sub):
    def in_proj(st, r0):
        h = _rms(read_x(st, r0), g_ref[...]).astype(BF16)
        st["z"] = _dot(h, win_ref[...])

    def normalise(st, r0):
        rs = slice(r0, r0 + sub)
        z = st.pop("z")
        q = _group_norm64(z[:, :TOK_W], gq_ref[...]) * (ATTN_SCALE * LOG2E)
        k = _group_norm64(z[:, TOK_W:2 * TOK_W], gk_ref[...])
        v = z[:, 2 * TOK_W:3 * TOK_W]
        for g in range(HEAD_GROUPS):
            cs = slice(g * GROUP_W, (g + 1) * GROUP_W)
            q_ref[g, rs, :] = q[:, cs].astype(BF16)
            k_ref[g, rs, :] = k[:, cs].astype(BF16)
            v_ref[g, rs, :] = v[:, cs].astype(BF16)
        qm_ref[rs, :] = _group_norm64(z[:, 3 * TOK_W:], gqm_ref[...]).astype(BF16)

    return [in_proj, normalise]


def _diff_outproj_stages(read_x, write_out, tok_ref, qm_ref, k_ref, v_ref, wout_ref, sub):
    def mem_attn(st, r0):
        st["mo"] = _mem_attention(qm_ref[r0:r0 + sub, :], k_ref, v_ref).astype(BF16)

    def out_proj(st, r0):
        cat = [tok_ref[g, r0:r0 + sub, :] for g in range(HEAD_GROUPS)] + [st.pop("mo")]
        y = _dot(jnp.concatenate(cat, axis=1), wout_ref[...])
        write_out(st, r0, read_x(st, r0) + y)

    return [mem_attn, out_proj]


def _run_stages(stages, rows, sub):
    _skewed_trace(stages, list(range(0, rows, sub)))


def _ffn_kernel(x_ref, fg_ref, fwin_ref, fwout_ref, o_ref):
    sub = SUB_FFN
    _run_stages(_ffn_stages(_read(x_ref, sub), _write(o_ref, sub), fg_ref, fwin_ref, fwout_ref),
                x_ref.shape[0], sub)


def _gmlp_kernel(x_ref, g_ref, win_ref, vg_ref, ws_ref, bst_ref, gq_ref, k_ref, v_ref,
                 wout_ref, o_ref):
    sub = SUB_GMLP
    _run_stages(_gmlp_stages(_read(x_ref, sub), _write(o_ref, sub), g_ref, win_ref, vg_ref,
                             ws_ref, bst_ref, gq_ref, k_ref, v_ref, wout_ref, sub),
                x_ref.shape[0], sub)


def _diff_inproj_kernel(x_ref, g_ref, win_ref, gq_ref, gk_ref, gqm_ref,
                        q_ref, k_ref, v_ref, qm_ref):
    sub = SUB_MIX
    _run_stages(_diff_inproj_stages(_read(x_ref, sub), g_ref, win_ref, gq_ref, gk_ref, gqm_ref,
                                    q_ref, k_ref, v_ref, qm_ref, sub), x_ref.shape[0], sub)


def _diff_outproj_kernel(x_ref, tok_ref, qm_ref, k_ref, v_ref, wout_ref, o_ref):
    sub = SUB_MIX
    _run_stages(_diff_outproj_stages(_read(x_ref, sub), _write(o_ref, sub), tok_ref, qm_ref,
                                     k_ref, v_ref, wout_ref, sub), x_ref.shape[0], sub)


def _cast_blocks(rows, n_steps):
    for blocks in range(n_steps, 0, -1):
        if n_steps % blocks == 0 and rows % (blocks * BF16_ROWS) == 0:
            return blocks
    raise ValueError(f"{rows} rows cannot be split into bf16 row blocks over {n_steps} steps")


def _cast_rider(stacked, lead, n_steps, flat_step):
    rows, cols = stacked.shape[len(lead):]
    blocks = _cast_blocks(rows, n_steps)
    steps_per_block = n_steps // blocks
    squeezed = (None,) * len(lead)

    def src_index(*g):
        return lead + (flat_step(*g) // steps_per_block, 0)

    def dst_index(*g):
        return (flat_step(*g) // steps_per_block, 0)

    src = pl.BlockSpec(squeezed + (rows // blocks, cols), src_index)
    dst = pl.BlockSpec((rows // blocks, cols), dst_index)
    return (stacked, src), (jax.ShapeDtypeStruct((rows, cols), BF16), dst)


def _with_riders(kernel_fn, n_in, n_out, n_riders):
    def kern(*refs):
        ins, rin = refs[:n_in], refs[n_in:n_in + n_riders]
        outs = refs[n_in + n_riders:n_in + n_riders + n_out]
        rout = refs[n_in + n_riders + n_out:n_in + 2 * n_riders + n_out]
        for src, dst in zip(rin, rout):
            dst[...] = src[...].astype(BF16)
        kernel_fn(*ins, *outs, *refs[n_in + 2 * n_riders + n_out:])
    return kern


def _ffn_weight_riders(ffn_w_in, ffn_w_out, lead, n_steps, flat_step):
    return [_cast_rider(ffn_w_in, lead, n_steps, flat_step),
            _cast_rider(ffn_w_out, lead, n_steps, flat_step)]


def _row_spec(tm, width):
    return pl.BlockSpec((tm, width), lambda i: (i, 0))


def _group_row_spec(tm):
    return pl.BlockSpec((HEAD_GROUPS, tm, GROUP_W), lambda i: (0, i, 0))


def _layer_const(stacked, lead):
    tail = stacked.shape[len(lead):]
    spec = pl.BlockSpec((None,) * len(lead) + tail, lambda *_: lead + (0,) * len(tail),
                        pipeline_mode=pl.Buffered(1))
    return (stacked, spec)


def _mem_spec(stacked, layer, seq, tm):
    tiles_per_row = seq // tm
    _, _, l, w = stacked.shape
    return (stacked, pl.BlockSpec((None, 1, l, w), lambda i: (layer, i // tiles_per_row, 0, 0)))


def _const(arr):
    return (arr, _const_spec(arr.shape))


def _token_call(kernel_fn, name, n, tm, ins, outs, riders=()):
    rin = [r[0] for r in riders]
    rout = [r[1] for r in riders]
    if riders:
        kernel_fn = _with_riders(kernel_fn, len(ins), len(outs), len(riders))
    return pl.pallas_call(
        kernel_fn,
        grid=(n // tm,),
        in_specs=[spec for _, spec in ins + rin],
        out_specs=[spec for _, spec in outs + rout],
        out_shape=[shape for shape, _ in outs + rout],
        compiler_params=pltpu.CompilerParams(
            dimension_semantics=("arbitrary",), vmem_limit_bytes=VMEM_LIMIT),
        name=name,
    )(*[arr for arr, _ in ins + rin])


def _ffn(x2d, gain, w_in, w_out, riders):
    n, d = x2d.shape
    return _token_call(
        _ffn_kernel, "ffn", n, TM_FFN,
        [(x2d, _row_spec(TM_FFN, d)), gain, _const(w_in), _const(w_out)],
        [(jax.ShapeDtypeStruct((n, d), F32), _row_spec(TM_FFN, d))], riders)


def _gmlp_mix(x2d, mix, kmem, vmem, w_o):
    n, d = x2d.shape
    (out,) = _token_call(
        _gmlp_kernel, "gmlp_mix", n, TM_GMLP,
        [(x2d, _row_spec(TM_GMLP, d))] + mix + [kmem, vmem, w_o],
        [(jax.ShapeDtypeStruct((n, d), F32), _row_spec(TM_GMLP, d))])
    return out


def _diff_inproj(x2d, mix):
    n, d = x2d.shape
    grouped = (jax.ShapeDtypeStruct((HEAD_GROUPS, n, GROUP_W), BF16), _group_row_spec(TM_MIX))
    return _token_call(
        _diff_inproj_kernel, "diff_inproj", n, TM_MIX, [(x2d, _row_spec(TM_MIX, d))] + mix,
        [grouped, grouped, grouped,
         (jax.ShapeDtypeStruct((n, MEM_W), BF16), _row_spec(TM_MIX, MEM_W))])


def _diff_outproj(x2d, tok, qm, kmem, vmem, w_o):
    n, d = x2d.shape
    (out,) = _token_call(
        _diff_outproj_kernel, "diff_outproj", n, TM_MIX,
        [(x2d, _row_spec(TM_MIX, d)), (tok, _group_row_spec(TM_MIX)),
         (qm, _row_spec(TM_MIX, MEM_W)), kmem, vmem, w_o],
        [(jax.ShapeDtypeStruct((n, d), F32), _row_spec(TM_MIX, d))])
    return out


def _memkv_kernel(mem_ref, g_ref, w_ref, gk_ref, k_ref, v_ref):
    mem = mem_ref[0]
    for i in range(w_ref.shape[0]):
        mh = _rms(mem, g_ref[i]).astype(BF16)
        kv = _dot(mh, w_ref[i])
        k = _group_norm64(kv[:, :MEM_W], gk_ref[i]) * ATTN_SCALE
        k_ref[i, 0] = k.astype(BF16)
        v_ref[i, 0] = kv[:, MEM_W:].astype(BF16)


def _memkv(mem, mem_norm, w_kv, gk_tiled, make_riders):
    depth = w_kv.shape[0]
    b, l, d = mem.shape
    kv_out = (jax.ShapeDtypeStruct((depth, b, l, MEM_W), BF16),
              pl.BlockSpec((depth, 1, l, MEM_W), lambda j: (0, j, 0, 0)))
    riders = make_riders(b, lambda j: j)
    ins = [(mem, pl.BlockSpec((1, l, d), lambda j: (j, 0, 0))),
           _const(mem_norm), _const(w_kv), _const(gk_tiled)]
    ins += [r[0] for r in riders]
    outs = [kv_out, kv_out] + [r[1] for r in riders]
    return pl.pallas_call(
        _with_riders(_memkv_kernel, 4, 2, len(riders)),
        grid=(b,),
        in_specs=[spec for _, spec in ins],
        out_specs=[spec for _, spec in outs],
        out_shape=[shape for shape, _ in outs],
        compiler_params=pltpu.CompilerParams(
            dimension_semantics=("arbitrary",), vmem_limit_bytes=VMEM_LIMIT),
        name="memkv",
    )(*[arr for arr, _ in ins])


def _split_components(q):
    lane = lax.broadcasted_iota(jnp.int32, (1, VAL_DIM), 1)
    zero = jnp.zeros_like(q)
    return jnp.concatenate([jnp.where(lane < HEAD_DIM, q, zero),
                            jnp.where(lane >= HEAD_DIM, q, zero)], axis=0)


def _causal_keep(rows, cols):
    r = lax.broadcasted_iota(jnp.int32, (rows, cols), 0)
    c = lax.broadcasted_iota(jnp.int32, (rows, cols), 1)
    return jnp.concatenate([c <= r, c <= r], axis=0)


def _diff_combine(o2, lam_ref, sg_ref, lambda_init):
    rows = o2.shape[0] // 2
    lp = lam_ref[...]
    lam = (jnp.exp(jnp.sum(lp[0:1] * lp[1:2], axis=-1, keepdims=True))
           - jnp.exp(jnp.sum(lp[2:3] * lp[3:4], axis=-1, keepdims=True)) + lambda_init)
    o = o2[:rows] - lam * o2[rows:]
    return (_rms(o, sg_ref[...]) * (1.0 - lambda_init)).astype(BF16)


def _diff_attn_online_kernel(lam_ref, sg_ref, q_ref, k_ref, v_ref, o_ref, *, lambda_init):
    qi = pl.program_id(2)
    qq = _split_components(q_ref[0])

    def scores(j):
        rows = pl.ds(pl.multiple_of(j * ATTN_TK, ATTN_TK), ATTN_TK)
        return _dot_nt(qq, k_ref[0, rows, :]), v_ref[0, rows, :]

    s, vb = scores(qi)
    s = jnp.where(_causal_keep(ATTN_TQ, ATTN_TK), s, -jnp.inf)
    m = jnp.max(s, axis=-1, keepdims=True)
    p = jnp.exp2(s - m)
    l = jnp.sum(p, axis=-1, keepdims=True)
    acc = _dot(p.astype(BF16), vb)

    def body(j, carry):
        m, l, acc = carry
        s, vb = scores(j)
        m_new = jnp.maximum(m, jnp.max(s, axis=-1, keepdims=True))
        alpha = jnp.exp2(m - m_new)
        p = jnp.exp2(s - m_new)
        l = alpha * l + jnp.sum(p, axis=-1, keepdims=True)
        acc = alpha * acc + _dot(p.astype(BF16), vb)
        return m_new, l, acc

    m, l, acc = lax.fori_loop(0, qi, body, (m, l, acc))
    o_ref[0] = _diff_combine(acc / l, lam_ref, sg_ref, lambda_init)


def _diff_attn_bounded_kernel(lam_ref, sg_ref, q_ref, k_ref, v_ref, o_ref, vext_ref,
                              *, lambda_init):
    s_len = q_ref.shape[1]
    row = lax.broadcasted_iota(jnp.int32, (VT_ROWS - VAL_DIM, s_len), 0)
    ones_row = jnp.where(row == 0, 1.0, 0.0).astype(BF16)
    bq = ATTN_BQ
    key_i = lax.broadcasted_iota(jnp.int32, (bq, bq), 0)
    qry_i = lax.broadcasted_iota(jnp.int32, (bq, bq), 1)
    keep = jnp.concatenate([key_i <= qry_i, key_i <= qry_i], axis=1)
    lp = lam_ref[...]
    lam = (jnp.exp(jnp.sum(lp[0:1] * lp[1:2], axis=-1, keepdims=True))
           - jnp.exp(jnp.sum(lp[2:3] * lp[3:4], axis=-1, keepdims=True)) + lambda_init)
    gain_t = sg_ref[...] * (1.0 - lambda_init)

    for hh in range(q_ref.shape[2] // VAL_DIM):
        hl = slice(hh * VAL_DIM, (hh + 1) * VAL_DIM)
        r0 = hh * VT_ROWS
        vext_ref[r0:r0 + VAL_DIM, :] = v_ref[0, :, hl].T
        vext_ref[r0 + VAL_DIM:r0 + VT_ROWS, :] = ones_row
        for i in range(s_len // bq):
            lo, hi = i * bq, (i + 1) * bq
            qq = _split_components(q_ref[0, lo:hi, hl])
            t = _dot_nt(k_ref[0, :hi, hl], qq)
            p = jnp.exp2(jnp.where(keep, t[lo:], -jnp.inf)).astype(BF16)
            if i > 0:
                p = jnp.concatenate([jnp.exp2(t[:lo]).astype(BF16), p], axis=0)
            acc = _dot(vext_ref[r0:r0 + VT_ROWS, :hi], p)
            o2 = acc[:VAL_DIM] * (1.0 / acc[VAL_DIM:VAL_DIM + 1])
            o = o2[:, :bq] - lam * o2[:, bq:]
            ms = jnp.mean(o * o, axis=0, keepdims=True)
            y = o * lax.rsqrt(ms + EPS) * gain_t
            o_ref[0, lo:hi, hl] = y.T.astype(BF16)


def _diff_attn(q, k, v, lam_p, subln_g, lambda_init, logit_bound):
    _, b, s, _ = q.shape
    out_shape = jax.ShapeDtypeStruct(q.shape, BF16)

    def bounded(q, k, v):
        head = pl.BlockSpec((None, 1, s, GROUP_W), lambda bi, h: (h, bi, 0, 0))
        return pl.pallas_call(
            functools.partial(_diff_attn_bounded_kernel, lambda_init=lambda_init),
            grid=(b, HEAD_GROUPS),
            in_specs=[_const_spec(lam_p.shape), _const_spec((VAL_DIM, 1)), head, head, head],
            out_specs=head,
            out_shape=out_shape,
            scratch_shapes=[pltpu.VMEM((ATTN_HEADS_PER_STEP * VT_ROWS, s), BF16)],
            compiler_params=pltpu.CompilerParams(
                dimension_semantics=("arbitrary", "arbitrary"), vmem_limit_bytes=VMEM_LIMIT),
            name="diff_attn_bounded",
        )(lam_p, subln_g.reshape(VAL_DIM, 1), q, k, v)

    def online(q, k, v):
        hps = ATTN_HEADS_PER_STEP
        head = pl.BlockSpec((None, 1, s, VAL_DIM), lambda bi, h, i: (h // hps, bi, 0, h % hps))
        blk = pl.BlockSpec((None, 1, ATTN_TQ, VAL_DIM),
                           lambda bi, h, i: (h // hps, bi, i, h % hps))
        return pl.pallas_call(
            functools.partial(_diff_attn_online_kernel, lambda_init=lambda_init),
            grid=(b, DIFF_HEADS, s // ATTN_TQ),
            in_specs=[_const_spec(lam_p.shape), _const_spec((1, VAL_DIM)), blk, head, head],
            out_specs=blk,
            out_shape=out_shape,
            compiler_params=pltpu.CompilerParams(
                dimension_semantics=("arbitrary", "arbitrary", "arbitrary"),
                vmem_limit_bytes=VMEM_LIMIT),
            name="diff_attn_online",
        )(lam_p, subln_g, q, k, v)

    return lax.cond(logit_bound <= FAST_MAX_LOG2, bounded, online, q, k, v)


def _tile_gain(g, reps):
    return jnp.tile(g.reshape(1, -1), (1, reps))


def kernel(x, mem, ffn_norm, ffn_w_in, ffn_w_out, mix_norm, mem_norm, w_mem_kv,
           memq_norm, memk_norm, w_out, a_w_in, a_v_norm, a_w_s, a_b_s,
           b_w_in, b_q_norm, b_k_norm, b_lambda, b_subln):
    batch, seq, d = x.shape
    depth = ffn_norm.shape[0]
    x2d = x.reshape(batch * seq, d)

    n_steps = (batch * seq) // TM_FFN
    ffn_order = [(i, half) for i in range(depth) for half in range(2)]
    ffn_gain = ffn_norm.reshape(depth, 2, 1, d)

    gk_mem = jnp.tile(memk_norm.reshape(depth, 1, HEAD_DIM), (1, 1, MEM_HEADS))
    kmem, vmem, w_in_bf, w_out_bf = _memkv(
        mem, mem_norm.reshape(depth, 1, d), w_mem_kv.astype(BF16), gk_mem,
        lambda steps, flat: _ffn_weight_riders(ffn_w_in, ffn_w_out, ffn_order[0], steps, flat))

    def ffn(x2d, pos, w_in_bf, w_out_bf, extra_riders=()):
        riders = []
        if pos + 1 < len(ffn_order):
            riders = _ffn_weight_riders(ffn_w_in, ffn_w_out, ffn_order[pos + 1], n_steps,
                                        lambda i: i)
        n_next = len(riders)
        res = _ffn(x2d, _layer_const(ffn_gain, ffn_order[pos]), w_in_bf, w_out_bf,
                   riders + list(extra_riders))
        nxt = res[1:1 + n_next] if n_next else [None, None]
        return res[0], nxt[0], nxt[1], res[1 + n_next:]

    for i in range(depth):
        j = i // N_MIXERS
        g_mix = _const(mix_norm[i].reshape(1, d))
        gq_mem = _const(_tile_gain(memq_norm[i], MEM_HEADS))
        tm = TM_GMLP if i % N_MIXERS == 0 else TM_MIX
        k_i, v_i = _mem_spec(kmem, i, seq, tm), _mem_spec(vmem, i, seq, tm)
        w_mix_in = a_w_in if i % N_MIXERS == 0 else b_w_in
        mix_riders = [_cast_rider(w_mix_in, (j,), n_steps, lambda s: s),
                      _cast_rider(w_out, (i,), n_steps, lambda s: s)]
        x2d, w_in_bf, w_out_bf, (w_mix_in_bf, w_o_bf) = ffn(x2d, 2 * i, w_in_bf, w_out_bf,
                                                            mix_riders)
        w_o = _const(w_o_bf)
        if i % N_MIXERS == 0:
            mix = [g_mix, _const(w_mix_in_bf), _const(a_v_norm[j].reshape(1, TOK_W)),
                   _const(a_w_s[j]), _const(a_b_s[j].T), gq_mem]
            x2d = _gmlp_mix(x2d, mix, k_i, v_i, w_o)
        else:
            lambda_init = 0.8 - 0.6 * math.exp(-0.3 * i)
            mix = [g_mix, _const(w_mix_in_bf),
                   _const(_tile_gain(b_q_norm[j], 2 * DIFF_HEADS)),
                   _const(_tile_gain(b_k_norm[j], 2 * DIFF_HEADS)), gq_mem]
            q, k, v, qm = _diff_inproj(x2d, mix)
            logit_bound = (1.01 * HEAD_DIM * ATTN_SCALE * LOG2E
                           * jnp.max(jnp.abs(b_q_norm[j])) * jnp.max(jnp.abs(b_k_norm[j])))
            per_row = (HEAD_GROUPS, batch, seq, GROUP_W)
            tok = _diff_attn(q.reshape(per_row), k.reshape(per_row), v.reshape(per_row),
                             b_lambda[j], b_subln[j].reshape(1, VAL_DIM), lambda_init,
                             logit_bound)
            x2d = _diff_outproj(x2d, tok.reshape(HEAD_GROUPS, batch * seq, GROUP_W), qm,
                                k_i, v_i, w_o)
        x2d, w_in_bf, w_out_bf, _ = ffn(x2d, 2 * i + 1, w_in_bf, w_out_bf)
    return x2d.reshape(batch, seq, d)
```
